```python
import math
import jax, jax.numpy as jnp
from jax import lax
import numpy as np

D_MODEL = 1024
BATCH = 32
SEQ = 256
DEPTH = 2
DEC_BATCH = 8
DEC_SEQ = 4096
PAST_LEN = 256

GRID_W = 64
N_MIXERS = 2
N_SSM_LAYERS = (DEPTH + 1) // 2
N_ATTN_LAYERS = DEPTH // 2
SSM_GROUP = 16
N_GROUPS = D_MODEL // SSM_GROUP
STATE_DIM = 64
DT_MIN = 1e-3
DT_MAX = 1e-1
HEAD_DIM = 64
N_HEADS = D_MODEL // (2 * HEAD_DIM)
V_DIM = 2 * HEAD_DIM
QBLK = 128
ROPE_THETA = 10000.0
N_EXPERTS = 256
TOP_K = 8
N_EXPERT_GROUPS = 8
TOPK_GROUPS = 4
D_EXPERT = 256
D_SHARED = 256
ROUTED_SCALE = 2.5
MOE_BLK = 128
EPS = 1e-6
F32 = jnp.float32

kernel_name = "hybrid_s5_diffattn_moe_diffusion_step"


def rmsnorm(x, g):
    xf = x.astype(F32)
    y = xf * lax.rsqrt(jnp.mean(xf * xf, axis=-1, keepdims=True) + EPS)
    return (y * g.astype(F32)).astype(x.dtype)


def ada_params(cond, w, b):
    m = jax.nn.silu(cond) @ w + b
    return jnp.split(m[:, None, :], 6, axis=-1)


def _cmul(ar, ai, br, bi):
    return ar * br - ai * bi, ar * bi + ai * br


def _ssm_combine(e1, e2):
    a1r, a1i, b1r, b1i = e1
    a2r, a2i, b2r, b2i = e2
    ar, ai = _cmul(a2r, a2i, a1r, a1i)
    br, bi = _cmul(a2r, a2i, b1r, b1i)
    return ar, ai, br + b2r, bi + b2i


def s5_direction(u, h0, lam_re, lam_im, log_dt, b_re, b_im, c_re, c_im, reverse):
    bt, L = u.shape[:2]
    dt = jnp.exp(log_dt)[:, None]
    mag = jnp.exp(lam_re * dt)
    a_re = mag * jnp.cos(lam_im * dt)
    a_im = mag * jnp.sin(lam_im * dt)
    den = lam_re * lam_re + lam_im * lam_im
    f_re = ((a_re - 1.0) * lam_re + a_im * lam_im) / den
    f_im = (a_im * lam_re - (a_re - 1.0) * lam_im) / den
    bb_re, bb_im = _cmul(f_re[..., None], f_im[..., None], b_re, b_im)
    bu_re = jnp.einsum('blgc,gpc->lbgp', u, bb_re)
    bu_im = jnp.einsum('blgc,gpc->lbgp', u, bb_im)
    a_re_l = jnp.broadcast_to(a_re[None, None], (L, 1) + a_re.shape)
    a_im_l = jnp.broadcast_to(a_im[None, None], (L, 1) + a_im.shape)
    acr, aci, hr, hi = lax.associative_scan(
        _ssm_combine, (a_re_l, a_im_l, bu_re, bu_im), reverse=reverse, axis=0)
    if h0 is not None:
        pr, pi = _cmul(acr, aci, h0[0][None], h0[1][None])
        hr = hr + pr
        hi = hi + pi
    y = jnp.einsum('lbgp,gcp->blgc', hr, c_re) - jnp.einsum('lbgp,gcp->blgc', hi, c_im)
    last = 0 if reverse else -1
    return y.reshape(bt, L, -1), jnp.stack([hr[last], hi[last]], axis=1)


def s5_mixer(h, h0s, p, j):
    bt, L, _ = h.shape
    u = h.astype(F32)
    ug = u.reshape(bt, L, N_GROUPS, SSM_GROUP)
    y = p['ssm_d'][j].astype(F32) * u
    finals = []
    for dr in range(2):
        h0 = None if h0s is None else (h0s[:, dr, 0].astype(F32), h0s[:, dr, 1].astype(F32))
        yd, fin = s5_direction(
            ug, h0,
            p['ssm_lam_re'][j, dr].astype(F32), p['ssm_lam_im'][j, dr].astype(F32),
            p['ssm_log_dt'][j, dr].astype(F32),
            p['ssm_b_re'][j, dr].astype(F32), p['ssm_b_im'][j, dr].astype(F32),
            p['ssm_c_re'][j, dr].astype(F32), p['ssm_c_im'][j, dr].astype(F32),
            reverse=(dr == 1))
        y = y + yd
        finals.append(fin)
    g = jax.nn.gelu(y).astype(h.dtype)
    a, b = jnp.split(g @ p['ssm_glu_w'][j], 2, axis=-1)
    return a * jax.nn.sigmoid(b), jnp.stack(finals, axis=1)


def rope_1d(x, pos):
    half = x.shape[-1] // 2
    freq = ROPE_THETA ** (-jnp.arange(half, dtype=F32) / half)
    ang = pos.astype(F32)[:, None] * freq[None, :]
    cos = jnp.cos(ang)[None, :, None, None, :]
    sin = jnp.sin(ang)[None, :, None, None, :]
    x1 = x[..., :half].astype(F32)
    x2 = x[..., half:].astype(F32)
    return jnp.concatenate([x1 * cos - x2 * sin, x2 * cos + x1 * sin], axis=-1).astype(x.dtype)


def axial_rope(x):
    L = x.shape[1]
    rows = L // GRID_W
    row = jnp.repeat(jnp.arange(rows), GRID_W)
    col = jnp.tile(jnp.arange(GRID_W), rows)
    r = HEAD_DIM // 2
    return jnp.concatenate([rope_1d(x[..., :r], row), rope_1d(x[..., r:], col)], axis=-1)


def diff_softmax_attention(q, k, v, lam):
    bt, lq = q.shape[:2]
    nb = lq // QBLK
    qb = jnp.moveaxis(q.reshape(bt, nb, QBLK, N_HEADS, 2, HEAD_DIM), 1, 0)
    scale = HEAD_DIM ** -0.5

    def block(qi):
        s = jnp.einsum('bqhmd,bkhmd->bhmqk', qi, k).astype(F32) * scale
        pr = jax.nn.softmax(s, axis=-1)
        w = pr[:, :, 0] - lam * pr[:, :, 1]
        return jnp.einsum('bhqk,bkhe->bqhe', w.astype(v.dtype), v)

    o = lax.map(block, qb)
    return jnp.moveaxis(o, 0, 1).reshape(bt, lq, N_HEADS, V_DIM)


def diff_attn_mixer(h, ctx_k, ctx_v, p, j, layer_idx):
    bt, L, _ = h.shape
    q, k, v = jnp.split(h @ p['attn_w_qkv'][j], 3, axis=-1)
    q = q.reshape(bt, L, N_HEADS, 2, HEAD_DIM)
    k = k.reshape(bt, L, N_HEADS, 2, HEAD_DIM)
    v = v.reshape(bt, L, N_HEADS, V_DIM)
    lam_init = 0.8 - 0.6 * math.exp(-0.3 * layer_idx)
    lq = p['attn_lam_q'][j].astype(F32)
    lk = p['attn_lam_k'][j].astype(F32)
    lam = jnp.exp(jnp.sum(lq[0] * lk[0])) - jnp.exp(jnp.sum(lq[1] * lk[1])) + lam_init
    if ctx_k is None:
        k_all, v_all = k, v
        q_use = q
    else:
        q_use = axial_rope(q)
        k_all = jnp.concatenate([axial_rope(k), ctx_k.astype(k.dtype)], axis=1)
        v_all = jnp.concatenate([v, ctx_v.astype(v.dtype)], axis=1)
    o = diff_softmax_attention(q_use, k_all, v_all, lam)
    o = rmsnorm(o, p['attn_subln_g'][j]) * (1.0 - lam_init)
    return o.reshape(bt, L, D_MODEL) @ p['attn_w_o'][j], k, v


def moe_routed(x, p, i):
    n, d = x.shape
    scores = jax.nn.sigmoid((x @ p['moe_router_w'][i]).astype(F32))
    biased = scores + p['moe_router_b'][i].astype(F32)
    grp = biased.reshape(n, N_EXPERT_GROUPS, N_EXPERTS // N_EXPERT_GROUPS)
    grp_score = jnp.sum(lax.top_k(grp, 2)[0], axis=-1)
    _, top_g = lax.top_k(grp_score, TOPK_GROUPS)
    gmask = jnp.sum(jax.nn.one_hot(top_g, N_EXPERT_GROUPS, dtype=F32), axis=1)
    emask = jnp.repeat(gmask, N_EXPERTS // N_EXPERT_GROUPS, axis=-1) > 0
    _, idx = lax.top_k(jnp.where(emask, biased, -jnp.inf), TOP_K)
    wts = jnp.take_along_axis(scores, idx, axis=-1)
    wts = wts / jnp.sum(wts, axis=-1, keepdims=True) * ROUTED_SCALE
    nk = n * TOP_K
    flat_e = idx.reshape(-1)
    order = jnp.argsort(flat_e)
    sorted_e = flat_e[order]
    counts = jnp.bincount(flat_e, length=N_EXPERTS)
    padded = (counts + MOE_BLK - 1) // MOE_BLK * MOE_BLK
    pad_end = jnp.cumsum(padded)
    pad_start = pad_end - padded
    start = jnp.cumsum(counts) - counts
    dest = pad_start[sorted_e] + jnp.arange(nk) - start[sorted_e]
    n_blocks = (nk + N_EXPERTS * (MOE_BLK - 1) + MOE_BLK - 1) // MOE_BLK
    rows = n_blocks * MOE_BLK
    row_tok = jnp.full((rows,), n, jnp.int32).at[dest].set((order // TOP_K).astype(jnp.int32))
    row_w = jnp.zeros((rows,), F32).at[dest].set(wts.reshape(-1)[order])
    blk_e = jnp.minimum(jnp.searchsorted(pad_end, jnp.arange(n_blocks) * MOE_BLK, side='right'),
                        N_EXPERTS - 1)
    x_pad = jnp.concatenate([x, jnp.zeros((1, d), x.dtype)], axis=0)
    w1 = p['moe_w1'][i]
    w3 = p['moe_w3'][i]
    w2 = p['moe_w2'][i]

    def expert_block(args):
        tok, w, e = args
        xb = x_pad[tok]
        hb = jax.nn.silu(xb @ w1[e]) * (xb @ w3[e])
        return (hb @ w2[e]) * w[:, None].astype(x.dtype)

    ys = lax.map(expert_block, (row_tok.reshape(n_blocks, MOE_BLK),
                                row_w.reshape(n_blocks, MOE_BLK), blk_e))
    return jax.ops.segment_sum(ys.reshape(rows, d), row_tok, num_segments=n + 1)[:n]


def moe(h, p, i):
    bt, L, d = h.shape
    xf = h.reshape(bt * L, d)
    sh = jax.nn.silu(xf @ p['moe_shared_w1'][i]) * (xf @ p['moe_shared_w3'][i])
    y = moe_routed(xf, p, i) + sh @ p['moe_shared_w2'][i]
    return y.reshape(bt, L, d)


def trunk(x, cond, p, state_ssm, cache_k, cache_v):
    is_ctx = state_ssm is None
    ssm_states, ks, vs = [], [], []
    for i in range(DEPTH):
        j = i // N_MIXERS
        sh1, sc1, g1, sh2, sc2, g2 = ada_params(cond, p['ada_w'][i], p['ada_b'][i])
        h = rmsnorm(x, p['norm1_g'][i]) * (1.0 + sc1) + sh1
        if i % N_MIXERS == 0:
            out, fin = s5_mixer(h, None if is_ctx else state_ssm[:, j], p, j)
            if is_ctx:
                ssm_states.append(fin)
        else:
            out, k, v = diff_attn_mixer(h, None if is_ctx else cache_k[:, j],
                                        None if is_ctx else cache_v[:, j], p, j, i)
            if is_ctx:
                ks.append(k)
                vs.append(v)
        x = x + g1 * out
        h = rmsnorm(x, p['norm2_g'][i]) * (1.0 + sc2) + sh2
        x = x + g2 * moe(h, p, i)
    return rmsnorm(x, p['final_g']), ssm_states, ks, vs


def setup_inputs(seed: int = 0) -> dict:
    key = jax.random.key(seed)
    ks = jax.random.split(key, 40)
    nrm = jax.random.normal
    D, G, P, CG = D_MODEL, N_GROUPS, STATE_DIM, SSM_GROUP
    NS, NA, E = N_SSM_LAYERS, N_ATTN_LAYERS, N_EXPERTS
    lam_im_base = math.pi * jnp.arange(P, dtype=F32)
    return {
        'x_prompt': nrm(ks[0], (BATCH, SEQ, D), F32),
        'x_sample': nrm(ks[1], (DEC_BATCH, DEC_SEQ, D), F32),
        'c': nrm(ks[2], (DEC_BATCH, D), F32),
        'state_ssm': 0.5 * nrm(ks[3], (DEC_BATCH, NS, 2, 2, G, P), F32),
        'cache_k': nrm(ks[4], (DEC_BATCH, NA, PAST_LEN, N_HEADS, 2, HEAD_DIM), F32),
        'cache_v': nrm(ks[5], (DEC_BATCH, NA, PAST_LEN, N_HEADS, V_DIM), F32),
        'c_ctx': nrm(ks[6], (D,), F32),
        'norm1_g': 1.0 + 0.02 * nrm(ks[7], (DEPTH, D), F32),
        'norm2_g': 1.0 + 0.02 * nrm(ks[8], (DEPTH, D), F32),
        'ada_w': 0.5 * D ** -0.5 * nrm(ks[9], (DEPTH, D, 6 * D), F32),
        'ada_b': 0.02 * nrm(ks[10], (DEPTH, 6 * D), F32),
        'ssm_lam_re': -0.5 + 0.01 * nrm(ks[11], (NS, 2, G, P), F32),
        'ssm_lam_im': lam_im_base + 0.01 * nrm(ks[12], (NS, 2, G, P), F32),
        'ssm_log_dt': jax.random.uniform(ks[13], (NS, 2, G), F32,
                                         minval=math.log(DT_MIN), maxval=math.log(DT_MAX)),
        'ssm_b_re': (2 * CG) ** -0.5 * nrm(ks[14], (NS, 2, G, P, CG), F32),
        'ssm_b_im': (2 * CG) ** -0.5 * nrm(ks[15], (NS, 2, G, P, CG), F32),
        'ssm_c_re': (2 * P) ** -0.5 * nrm(ks[16], (NS, 2, G, CG, P), F32),
        'ssm_c_im': (2 * P) ** -0.5 * nrm(ks[17], (NS, 2, G, CG, P), F32),
        'ssm_d': nrm(ks[18], (NS, D), F32),
        'ssm_glu_w': D ** -0.5 * nrm(ks[19], (NS, D, 2 * D), F32),
        'attn_w_qkv': D ** -0.5 * nrm(ks[20], (NA, D, 3 * D), F32),
        'attn_lam_q': 0.1 * nrm(ks[21], (NA, 2, HEAD_DIM), F32),
        'attn_lam_k': 0.1 * nrm(ks[22], (NA, 2, HEAD_DIM), F32),
        'attn_subln_g': 1.0 + 0.02 * nrm(ks[23], (NA, V_DIM), F32),
        'attn_w_o': D ** -0.5 * nrm(ks[24], (NA, D, D), F32),
        'moe_router_w': D ** -0.5 * nrm(ks[25], (DEPTH, D, E), F32),
        'moe_router_b': 0.01 * nrm(ks[26], (DEPTH, E), F32),
        'moe_w1': D ** -0.5 * nrm(ks[27], (DEPTH, E, D, D_EXPERT), F32),
        'moe_w3': D ** -0.5 * nrm(ks[28], (DEPTH, E, D, D_EXPERT), F32),
        'moe_w2': D_EXPERT ** -0.5 * nrm(ks[29], (DEPTH, E, D_EXPERT, D), F32),
        'moe_shared_w1': D ** -0.5 * nrm(ks[30], (DEPTH, D, D_SHARED), F32),
        'moe_shared_w3': D ** -0.5 * nrm(ks[31], (DEPTH, D, D_SHARED), F32),
        'moe_shared_w2': D_SHARED ** -0.5 * nrm(ks[32], (DEPTH, D_SHARED, D), F32),
        'final_g': 1.0 + 0.02 * nrm(ks[33], (D,), F32),
    }


def reference(x_prompt, x_sample, c, state_ssm, cache_k, cache_v, c_ctx,
              norm1_g, norm2_g, ada_w, ada_b,
              ssm_lam_re, ssm_lam_im, ssm_log_dt, ssm_b_re, ssm_b_im, ssm_c_re, ssm_c_im,
              ssm_d, ssm_glu_w,
              attn_w_qkv, attn_lam_q, attn_lam_k, attn_subln_g, attn_w_o,
              moe_router_w, moe_router_b, moe_w1, moe_w3, moe_w2,
              moe_shared_w1, moe_shared_w3, moe_shared_w2, final_g):
    p = dict(norm1_g=norm1_g, norm2_g=norm2_g, ada_w=ada_w, ada_b=ada_b,
             ssm_lam_re=ssm_lam_re, ssm_lam_im=ssm_lam_im, ssm_log_dt=ssm_log_dt,
             ssm_b_re=ssm_b_re, ssm_b_im=ssm_b_im, ssm_c_re=ssm_c_re, ssm_c_im=ssm_c_im,
             ssm_d=ssm_d, ssm_glu_w=ssm_glu_w,
             attn_w_qkv=attn_w_qkv, attn_lam_q=attn_lam_q, attn_lam_k=attn_lam_k,
             attn_subln_g=attn_subln_g, attn_w_o=attn_w_o,
             moe_router_w=moe_router_w, moe_router_b=moe_router_b,
             moe_w1=moe_w1, moe_w3=moe_w3, moe_w2=moe_w2,
             moe_shared_w1=moe_shared_w1, moe_shared_w3=moe_shared_w3,
             moe_shared_w2=moe_shared_w2, final_g=final_g)
    y_prompt, ssm_list, k_list, v_list = trunk(x_prompt, c_ctx[None, :], p, None, None, None)
    new_state_ssm = jnp.stack(ssm_list, axis=1)
    new_cache_k = jnp.stack(k_list, axis=1)
    new_cache_v = jnp.stack(v_list, axis=1)
    y_sample, _, _, _ = trunk(x_sample, c, p, state_ssm, cache_k, cache_v)
    return (y_prompt, y_sample, new_state_ssm, new_cache_k, new_cache_v)
```

```python
import functools
import math

import jax
import jax.numpy as jnp
from jax import lax
from jax.experimental import pallas as pl
from jax.experimental.pallas import tpu as pltpu

F32 = jnp.float32
BF16 = jnp.bfloat16
I32 = jnp.int32

EPS = 1e-6
SSM_GROUP = 16
CHUNK = 16
GRID_W = 64
HEAD_DIM = 64
V_DIM = 2 * HEAD_DIM
ROPE_THETA = 10000.0
TOP_K = 8
N_EXPERT_GROUPS = 8
TOPK_GROUPS = 4
ROUTED_SCALE = 2.5
N_MOD = 6

LANES = 128
VMEM_LIMIT = 48 * 1024 * 1024

TM_DENSE = 512
TM_ROUTER = 256
TM_EXPERT = 256
TM_ROWS = 128
TQ_ATTN = 256


def _params(sem):
    return pltpu.CompilerParams(dimension_semantics=sem, vmem_limit_bytes=VMEM_LIMIT)


def _normmod(x, g, sc, sh):
    ms = jnp.mean(x * x, axis=-1, keepdims=True)
    return (x * lax.rsqrt(ms + EPS) * g) * (1.0 + sc) + sh


def _cond_map(tm, n_ctx, dec_seq):
    t_ctx = n_ctx // tm
    per = dec_seq // tm

    def f(i):
        return jnp.where(i < t_ctx, 0, 1 + (i - t_ctx) // per)
    return f


def _ada_kernel(c_ref, w_ref, b_ref, o_ref):
    c = c_ref[...]
    s = (c * jax.nn.sigmoid(c)).astype(BF16)
    o_ref[0] = jnp.dot(s, w_ref[0].astype(BF16), preferred_element_type=F32) + b_ref[0]


def _ada(cond, ada_w, ada_b):
    depth, d, n6 = ada_w.shape
    nc = cond.shape[0]
    tn = 1536
    out = pl.pallas_call(
        _ada_kernel,
        grid=(depth, n6 // tn),
        in_specs=[pl.BlockSpec((nc, d), lambda l, j: (0, 0)),
                  pl.BlockSpec((1, d, tn), lambda l, j: (l, 0, j)),
                  pl.BlockSpec((1, 1, tn), lambda l, j: (l, 0, j))],
        out_specs=pl.BlockSpec((1, nc, tn), lambda l, j: (l, 0, j)),
        out_shape=jax.ShapeDtypeStruct((depth, nc, n6), F32),
        compiler_params=_params(("arbitrary", "arbitrary")),
        name="ada",
    )(cond, ada_w, ada_b.reshape(depth, 1, n6))
    return out.reshape(depth, nc, N_MOD, d)


def _normmod_kernel(x_ref, mod_ref, g_ref, o_ref):
    m = mod_ref[0]
    o_ref[...] = _normmod(x_ref[...], g_ref[...], m[1:2], m[0:1]).astype(o_ref.dtype)


def _norm1(x, mod, g, cmap):
    n, d = x.shape
    tm = TM_DENSE
    return pl.pallas_call(
        _normmod_kernel,
        grid=(n // tm,),
        in_specs=[pl.BlockSpec((tm, d), lambda i: (i, 0)),
                  pl.BlockSpec((1, N_MOD, d), lambda i: (cmap(i), 0, 0)),
                  pl.BlockSpec((1, d), lambda i: (0, 0))],
        out_specs=pl.BlockSpec((tm, d), lambda i: (i, 0)),
        out_shape=jax.ShapeDtypeStruct((n, d), BF16),
        compiler_params=_params(("arbitrary",)),
        name="norm1",
    )(x, mod, g.reshape(1, d))


def _s5_weights(lam_re, lam_im, log_dt, b_re, b_im, c_re, c_im):
    hi = lax.Precision.HIGHEST
    t = CHUNK
    w_in, w_out, toep, a16 = [], [], [], []
    for dr in range(2):
        dt = jnp.exp(log_dt[dr])[:, None]
        lr, li = lam_re[dr], lam_im[dr]
        mag = jnp.exp(lr * dt)
        a_re = mag * jnp.cos(li * dt)
        a_im = mag * jnp.sin(li * dt)
        den = lr * lr + li * li
        f_re = ((a_re - 1.0) * lr + a_im * li) / den
        f_im = (a_im * lr - (a_re - 1.0) * li) / den
        bb_re = f_re[..., None] * b_re[dr] - f_im[..., None] * b_im[dr]
        bb_im = f_re[..., None] * b_im[dr] + f_im[..., None] * b_re[dr]
        k = jnp.arange(t + 1, dtype=F32)[:, None, None]
        pm = jnp.exp(k * (lr * dt)[None])
        pk_re = pm * jnp.cos(k * (li * dt)[None])
        pk_im = pm * jnp.sin(k * (li * dt)[None])
        ks = jnp.arange(t)
        sel = (t - 1 - ks) if dr == 0 else ks
        pr, pi = pk_re[sel], pk_im[sel]
        wi_re = pr[..., None] * bb_re[None] - pi[..., None] * bb_im[None]
        wi_im = pr[..., None] * bb_im[None] + pi[..., None] * bb_re[None]
        g_, p_, c_ = bb_re.shape
        w_in.append((wi_re.transpose(1, 0, 3, 2).reshape(g_, t * c_, p_),
                     wi_im.transpose(1, 0, 3, 2).reshape(g_, t * c_, p_)))
        sel = (ks + 1) if dr == 0 else (t - ks)
        pr, pi = pk_re[sel], pk_im[sel]
        ca_re = c_re[dr][None] * pr[:, :, None, :] - c_im[dr][None] * pi[:, :, None, :]
        ca_im = c_re[dr][None] * pi[:, :, None, :] + c_im[dr][None] * pr[:, :, None, :]
        w_out.append((ca_re.transpose(1, 3, 0, 2).reshape(g_, p_, t * c_),
                      (-ca_im).transpose(1, 3, 0, 2).reshape(g_, p_, t * c_)))
        pr, pi = pk_re[:t], pk_im[:t]
        cd_re = c_re[dr][None] * pr[:, :, None, :] - c_im[dr][None] * pi[:, :, None, :]
        cd_im = c_re[dr][None] * pi[:, :, None, :] + c_im[dr][None] * pr[:, :, None, :]
        kk = (jnp.einsum('kgop,gpi->kgoi', cd_re, bb_re, precision=hi)
              - jnp.einsum('kgop,gpi->kgoi', cd_im, bb_im, precision=hi))
        s_i = ks[:, None]
        t_i = ks[None, :]
        delta = (t_i - s_i) if dr == 0 else (s_i - t_i)
        kd = jnp.where((delta >= 0)[:, :, None, None, None], kk[jnp.clip(delta, 0, t - 1)], 0.0)
        toep.append(kd.transpose(2, 0, 4, 1, 3).reshape(g_, t * c_, t * c_))
        a16.append((pk_re[t], pk_im[t]))
    w_in_all = jnp.concatenate([w_in[0][0], w_in[1][0], w_in[0][1], w_in[1][1]], axis=2)
    w_out_all = jnp.concatenate([w_out[0][0], w_out[1][0], w_out[0][1], w_out[1][1]], axis=1)
    toep_all = toep[0] + toep[1]
    a16_all = jnp.stack([jnp.concatenate([a16[0][0], a16[1][0]], axis=1),
                         jnp.concatenate([a16[0][1], a16[1][1]], axis=1)], axis=1)
    return w_in_all.astype(BF16), toep_all.astype(BF16), w_out_all.astype(BF16), a16_all


def _s5_kernel(u_ref, win_ref, wtp_ref, wout_ref, a16_ref, h0_ref, y_ref, fin_ref, v_s, sf_s, sb_s, *, nc, nb):
    u = u_ref[...]
    v_s[...] = jnp.dot(u, win_ref[0], preferred_element_type=F32)
    a = a16_ref[0]
    ar, ai = a[0:1], a[1:2]
    fwd = (lax.broadcasted_iota(I32, (nb, 2 * LANES), 1) % LANES) < (LANES // 2)

    def step(i, carry):
        sr, si = carry
        st = jnp.concatenate([sr, si], axis=1)
        rf = pl.multiple_of(i * nb, nb)
        rb = pl.multiple_of((nc - 1 - i) * nb, nb)
        sf_s[pl.ds(rf, nb), :] = st
        sb_s[pl.ds(rb, nb), :] = st
        v = jnp.where(fwd, v_s[pl.ds(rf, nb), :], v_s[pl.ds(rb, nb), :])
        vr, vi = v[:, :LANES], v[:, LANES:]
        return ar * sr - ai * si + vr, ar * si + ai * sr + vi

    h0 = h0_ref[0]
    sr, si = lax.fori_loop(0, nc, step, (h0[:, :LANES], h0[:, LANES:]))
    fin_ref[0] = jnp.concatenate([sr, si], axis=1)
    fwd_all = (lax.broadcasted_iota(I32, (nc * nb, 2 * LANES), 1) % LANES) < (LANES // 2)
    hs = jnp.where(fwd_all, sf_s[...], sb_s[...]).astype(BF16)
    y_ref[...] = (jnp.dot(u, wtp_ref[0], preferred_element_type=F32)
                  + jnp.dot(hs, wout_ref[0], preferred_element_type=F32))


def _s5(h_seq, h0, wts):
    w_in, w_tp, w_out, a16 = wts
    nb, l, d = h_seq.shape
    g = d // SSM_GROUP
    nc = l // CHUNK
    r = nc * nb
    kw = CHUNK * SSM_GROUP
    u = h_seq.reshape(nb, nc, CHUNK, g, SSM_GROUP).transpose(1, 0, 3, 2, 4).reshape(r, g * kw)
    y, fin = pl.pallas_call(
        functools.partial(_s5_kernel, nc=nc, nb=nb),
        grid=(g,),
        in_specs=[pl.BlockSpec((r, kw), lambda j: (0, j)),
                  pl.BlockSpec((1, kw, kw), lambda j: (j, 0, 0)),
                  pl.BlockSpec((1, kw, kw), lambda j: (j, 0, 0)),
                  pl.BlockSpec((1, kw, kw), lambda j: (j, 0, 0)),
                  pl.BlockSpec((1, 2, LANES), lambda j: (j, 0, 0)),
                  pl.BlockSpec((1, nb, kw), lambda j: (j, 0, 0))],
        out_specs=[pl.BlockSpec((r, kw), lambda j: (0, j)),
                   pl.BlockSpec((1, nb, kw), lambda j: (j, 0, 0))],
        out_shape=[jax.ShapeDtypeStruct((r, g * kw), F32),
                   jax.ShapeDtypeStruct((g, nb, kw), F32)],
        scratch_shapes=[pltpu.VMEM((r, kw), F32)] * 3,
        compiler_params=_params(("arbitrary",)),
        name="s5_scan",
    )(u, w_in, w_tp, w_out, a16, h0)
    y = y.reshape(nc, nb, g, CHUNK, SSM_GROUP).transpose(1, 0, 3, 2, 4).reshape(nb * l, d)
    return y, fin


def _glu_kernel(x_ref, y_ref, mod_ref, g_ref, d_ref, w_ref, o_ref):
    m = mod_ref[0]
    x = x_ref[...]
    h = _normmod(x, g_ref[...], m[1:2], m[0:1])
    y = d_ref[...] * h + y_ref[...]
    gl = jax.nn.gelu(y).astype(BF16)
    z = jnp.dot(gl, w_ref[...], preferred_element_type=F32)
    dd = x.shape[1]
    o_ref[...] = x + m[2:3] * (z[:, :dd] * jax.nn.sigmoid(z[:, dd:]))


def _glu(x, y, mod, g, dskip, w, cmap):
    n, d = x.shape
    tm = TM_DENSE
    return pl.pallas_call(
        _glu_kernel,
        grid=(n // tm,),
        in_specs=[pl.BlockSpec((tm, d), lambda i: (i, 0)),
                  pl.BlockSpec((tm, d), lambda i: (i, 0)),
                  pl.BlockSpec((1, N_MOD, d), lambda i: (cmap(i), 0, 0)),
                  pl.BlockSpec((1, d), lambda i: (0, 0)),
                  pl.BlockSpec((1, d), lambda i: (0, 0)),
                  pl.BlockSpec((d, 2 * d), lambda i: (0, 0))],
        out_specs=pl.BlockSpec((tm, d), lambda i: (i, 0)),
        out_shape=jax.ShapeDtypeStruct((n, d), F32),
        compiler_params=_params(("arbitrary",)),
        name="s5_glu",
    )(x, y, mod, g.reshape(1, d), dskip.reshape(1, d), w)


def _rope_tables(l):
    r = HEAD_DIM // 2
    half = r // 2
    freq = ROPE_THETA ** (-jnp.arange(half, dtype=F32) / half)
    pos = jnp.arange(l)
    row = (pos // GRID_W).astype(F32)
    col = (pos % GRID_W).astype(F32)
    d = jnp.arange(LANES) % HEAD_DIM
    p = jnp.where((d < r)[None, :], row[:, None], col[:, None])
    ang = p * freq[d % half][None, :]
    sign = jnp.where((d % r) < half, -1.0, 1.0)[None, :]
    return jnp.cos(ang), jnp.sin(ang) * sign


def _qkv_kernel(*refs, rope):
    if rope:
        x_ref, mod_ref, g_ref, w_ref, cos_ref, sin_ref, q_ref, k_ref, v_ref = refs
    else:
        x_ref, mod_ref, g_ref, w_ref, q_ref, k_ref, v_ref, kf_ref, vf_ref = refs
    m = mod_ref[0]
    h = _normmod(x_ref[...], g_ref[...], m[1:2], m[0:1]).astype(BF16)
    qkv = jnp.dot(h, w_ref[...], preferred_element_type=F32)
    d = x_ref.shape[1]
    q = qkv[:, :d] * (HEAD_DIM ** -0.5)
    k = qkv[:, d:2 * d]
    v = qkv[:, 2 * d:]
    v_ref[...] = v.astype(BF16)
    if not rope:
        q_ref[...] = q.astype(BF16)
        k_ref[...] = k.astype(BF16)
        kf_ref[...] = k
        vf_ref[...] = v
        return
    cos = cos_ref[...]
    sin = sin_ref[...]
    half = HEAD_DIM // 4
    lane = lax.broadcasted_iota(I32, cos.shape, 1)
    up_sel = (lane % (2 * half)) < half
    for src, dst in ((q, q_ref), (k, k_ref)):
        for j in range(d // LANES):
            blk = src[:, j * LANES:(j + 1) * LANES]
            up = pltpu.roll(blk, LANES - half, 1)
            dn = pltpu.roll(blk, half, 1)
            dst[:, j * LANES:(j + 1) * LANES] = (blk * cos + jnp.where(up_sel, up, dn) * sin).astype(BF16)


def _qkv(x, mod, g, w, cmap, row0, nrows, rope_l=None):
    n, d = x.shape
    tm = TM_DENSE
    t0 = row0 // tm
    rope = rope_l is not None
    in_specs = [pl.BlockSpec((tm, d), lambda i: (i + t0, 0)),
                pl.BlockSpec((1, N_MOD, d), lambda i: (cmap(i + t0), 0, 0)),
                pl.BlockSpec((1, d), lambda i: (0, 0)),
                pl.BlockSpec((d, 3 * d), lambda i: (0, 0))]
    args = [x, mod, g.reshape(1, d), w]
    row_spec = pl.BlockSpec((tm, d), lambda i: (i, 0))
    out_specs = [row_spec] * 3
    out_shape = [jax.ShapeDtypeStruct((nrows, d), BF16)] * 3
    if rope:
        per = rope_l // tm
        cos, sin = _rope_tables(rope_l)
        in_specs += [pl.BlockSpec((tm, LANES), lambda i: (i % per, 0))] * 2
        args += [cos, sin]
    else:
        out_specs += [row_spec] * 2
        out_shape += [jax.ShapeDtypeStruct((nrows, d), F32)] * 2
    return pl.pallas_call(
        functools.partial(_qkv_kernel, rope=rope),
        grid=(nrows // tm,),
        in_specs=in_specs,
        out_specs=out_specs,
        out_shape=out_shape,
        compiler_params=_params(("arbitrary",)),
        name="qkv_rope" if rope else "qkv",
    )(*args)


def _attn_kernel(q_ref, kt_ref, v_ref, lq_ref, lk_ref, g_ref, o_ref, *, lam_init):
    q = q_ref[0]
    tq = q.shape[0]
    lane = lax.broadcasted_iota(I32, q.shape, 1)
    zero = jnp.zeros_like(q)
    qq = jnp.concatenate([jnp.where(lane < HEAD_DIM, q, zero), jnp.where(lane >= HEAD_DIM, q, zero)], axis=0)
    s = jnp.dot(qq, kt_ref[0, 0], preferred_element_type=F32)
    p = jnp.exp(s - jnp.max(s, axis=1, keepdims=True))
    l = jnp.sum(p, axis=1, keepdims=True)
    lqk = lq_ref[...] * lk_ref[...]
    lam = (jnp.exp(jnp.sum(lqk[0:1], axis=1, keepdims=True))
           - jnp.exp(jnp.sum(lqk[1:2], axis=1, keepdims=True)) + lam_init)
    w = p[:tq] * (1.0 / l[:tq]) - p[tq:] * (lam / l[tq:])
    o = jnp.dot(w.astype(BF16), v_ref[0], preferred_element_type=F32)
    o = o * lax.rsqrt(jnp.mean(o * o, axis=1, keepdims=True) + EPS) * g_ref[...]
    o_ref[0] = (o * (1.0 - lam_init)).astype(o_ref.dtype)


def _attention(q, k, v, lam_q, lam_k, subln_g, lam_init):
    nb, l, d = q.shape
    lk = k.shape[1]
    nh = d // V_DIM
    kt = k.reshape(nb, lk, nh, V_DIM).transpose(0, 2, 3, 1)
    tq = TQ_ATTN
    return pl.pallas_call(
        functools.partial(_attn_kernel, lam_init=lam_init),
        grid=(nb, nh, l // tq),
        in_specs=[pl.BlockSpec((1, tq, V_DIM), lambda b, h, i: (b, i, h)),
                  pl.BlockSpec((1, 1, V_DIM, lk), lambda b, h, i: (b, h, 0, 0)),
                  pl.BlockSpec((1, lk, V_DIM), lambda b, h, i: (b, 0, h)),
                  pl.BlockSpec((2, HEAD_DIM), lambda b, h, i: (0, 0)),
                  pl.BlockSpec((2, HEAD_DIM), lambda b, h, i: (0, 0)),
                  pl.BlockSpec((1, V_DIM), lambda b, h, i: (0, 0))],
        out_specs=pl.BlockSpec((1, tq, V_DIM), lambda b, h, i: (b, i, h)),
        out_shape=jax.ShapeDtypeStruct((nb, l, d), BF16),
        compiler_params=_params(("arbitrary", "arbitrary", "arbitrary")),
        name="diff_attn",
    )(q, kt, v, lam_q, lam_k, subln_g.reshape(1, V_DIM))


def _oproj_kernel(x_ref, o_ref, mod_ref, w_ref, y_ref):
    m = mod_ref[0]
    y_ref[...] = x_ref[...] + m[2:3] * jnp.dot(o_ref[...], w_ref[...], preferred_element_type=F32)


def _oproj(x, o, mod, w, cmap):
    n, d = x.shape
    tm = TM_DENSE
    return pl.pallas_call(
        _oproj_kernel,
        grid=(n // tm,),
        in_specs=[pl.BlockSpec((tm, d), lambda i: (i, 0)),
                  pl.BlockSpec((tm, d), lambda i: (i, 0)),
                  pl.BlockSpec((1, N_MOD, d), lambda i: (cmap(i), 0, 0)),
                  pl.BlockSpec((d, d), lambda i: (0, 0))],
        out_specs=pl.BlockSpec((tm, d), lambda i: (i, 0)),
        out_shape=jax.ShapeDtypeStruct((n, d), F32),
        compiler_params=_params(("arbitrary",)),
        name="attn_out",
    )(x, o, mod, w)


def _stack_rows(rows, width):
    ri = lax.broadcasted_iota(I32, (len(rows), width), 0)
    out = jnp.zeros((len(rows), width), rows[0].dtype)
    for j, r in enumerate(rows):
        out = jnp.where(ri == j, r, out)
    return out


def _router_kernel(x_ref, mod_ref, g_ref, wh_ref, wl_ref, b_ref, h_ref, idx_ref, w_ref, pos_ref, cnt_ref, cnt_s):
    step = pl.program_id(0)

    @pl.when(step == 0)
    def _():
        cnt_s[...] = jnp.zeros_like(cnt_s)

    m = mod_ref[0]
    h = _normmod(x_ref[...], g_ref[...], m[4:5], m[3:4])
    h_ref[...] = h
    tr = h.shape[0]
    ne = wh_ref.shape[0]
    per = ne // N_EXPERT_GROUPS
    hh = h.astype(BF16)
    hl = (h - hh.astype(F32)).astype(BF16)
    nt = (((1,), (1,)), ((), ()))
    wh = wh_ref[...]
    logits = (lax.dot_general(wh, hh, nt, preferred_element_type=F32)
              + lax.dot_general(wl_ref[...], hh, nt, preferred_element_type=F32)
              + lax.dot_general(wh, hl, nt, preferred_element_type=F32))
    scores = jax.nn.sigmoid(logits)
    biased = scores + b_ref[...]
    neg = -jnp.inf
    ri = lax.broadcasted_iota(I32, (per, tr), 0).astype(F32)
    blocks, gscores = [], []
    for j in range(N_EXPERT_GROUPS):
        blk = biased[j * per:(j + 1) * per, :]
        m1 = jnp.max(blk, axis=0, keepdims=True)
        f1 = jnp.min(jnp.where(blk == m1, ri, float(per)), axis=0, keepdims=True)
        m2 = jnp.max(jnp.where(ri == f1, neg, blk), axis=0, keepdims=True)
        blocks.append(blk)
        gscores.append(m1 + m2)
    gs = _stack_rows(gscores, tr)
    gi = lax.broadcasted_iota(I32, gs.shape, 0).astype(F32)
    gsel = jnp.zeros(gs.shape, F32)
    for _ in range(TOPK_GROUPS):
        mm = jnp.max(gs, axis=0, keepdims=True)
        pick = gi == jnp.min(jnp.where(gs == mm, gi, float(N_EXPERT_GROUPS)), axis=0, keepdims=True)
        gsel = jnp.where(pick, 1.0, gsel)
        gs = jnp.where(pick, neg, gs)
    masked = jnp.concatenate(
        [jnp.where(gsel[j:j + 1, :] > 0.0, blocks[j], neg) for j in range(N_EXPERT_GROUPS)], axis=0)
    ei = lax.broadcasted_iota(I32, masked.shape, 0).astype(F32)
    picks, idxs, ws = [], [], []
    for _ in range(TOP_K):
        mm = jnp.max(masked, axis=0, keepdims=True)
        ff = jnp.min(jnp.where(masked == mm, ei, float(ne)), axis=0, keepdims=True)
        pick = ei == ff
        picks.append(pick)
        idxs.append(ff)
        ws.append(jnp.sum(jnp.where(pick, scores, 0.0), axis=0, keepdims=True))
        masked = jnp.where(pick, neg, masked)
    wsum = ws[0]
    for wk in ws[1:]:
        wsum = wsum + wk
    w_ref[...] = _stack_rows([wk / wsum * ROUTED_SCALE for wk in ws], tr)
    idx_ref[...] = _stack_rows(idxs, tr).astype(I32)
    sel = jnp.zeros(masked.shape, F32)
    for pick in picks:
        sel = sel + jnp.where(pick, 1.0, 0.0)
    earlier = (lax.broadcasted_iota(I32, (tr, tr), 0) < lax.broadcasted_iota(I32, (tr, tr), 1))
    prefix = jnp.dot(sel.astype(BF16), jnp.where(earlier, 1.0, 0.0).astype(BF16), preferred_element_type=F32)
    base = prefix + cnt_s[...]
    pos_ref[...] = _stack_rows(
        [jnp.sum(jnp.where(pick, base, 0.0), axis=0, keepdims=True) for pick in picks], tr).astype(I32)
    cnt_s[...] = cnt_s[...] + jnp.sum(sel, axis=1, keepdims=True)
    cnt_ref[...] = cnt_s[...]


def _router(x, mod, g, w_t_hi, w_t_lo, bias, cmap):
    n, d = x.shape
    ne = w_t_hi.shape[0]
    tr = TM_ROUTER
    tok_spec = pl.BlockSpec((TOP_K, tr), lambda i: (0, i))
    return pl.pallas_call(
        _router_kernel,
        grid=(n // tr,),
        in_specs=[pl.BlockSpec((tr, d), lambda i: (i, 0)),
                  pl.BlockSpec((1, N_MOD, d), lambda i: (cmap(i), 0, 0)),
                  pl.BlockSpec((1, d), lambda i: (0, 0)),
                  pl.BlockSpec((ne, d), lambda i: (0, 0)),
                  pl.BlockSpec((ne, d), lambda i: (0, 0)),
                  pl.BlockSpec((ne, 1), lambda i: (0, 0))],
        out_specs=[pl.BlockSpec((tr, d), lambda i: (i, 0)), tok_spec, tok_spec, tok_spec,
                   pl.BlockSpec((ne, 1), lambda i: (0, 0))],
        out_shape=[jax.ShapeDtypeStruct((n, d), F32),
                   jax.ShapeDtypeStruct((TOP_K, n), I32),
                   jax.ShapeDtypeStruct((TOP_K, n), F32),
                   jax.ShapeDtypeStruct((TOP_K, n), I32),
                   jax.ShapeDtypeStruct((ne, 1), F32)],
        scratch_shapes=[pltpu.VMEM((ne, 1), F32)],
        compiler_params=_params(("arbitrary",)),
        name="moe_router",
    )(x, mod, g.reshape(1, d), w_t_hi, w_t_lo, bias.reshape(ne, 1))


def _dest_kernel(idx_ref, pos_ref, start_ref, o_ref):
    idx = idx_ref[...]
    ei = lax.broadcasted_iota(I32, (start_ref.shape[0], idx.shape[1]), 0)
    start = start_ref[...]
    rows = [jnp.sum(jnp.where(ei == idx[k:k + 1], start, 0.0), axis=0, keepdims=True) for k in range(TOP_K)]
    o_ref[...] = _stack_rows(rows, idx.shape[1]).astype(I32) + pos_ref[...]


def _dest(idx, pos, start):
    n = idx.shape[1]
    ne = start.shape[0]
    tr = 512
    tok_spec = pl.BlockSpec((TOP_K, tr), lambda i: (0, i))
    return pl.pallas_call(
        _dest_kernel,
        grid=(n // tr,),
        in_specs=[tok_spec, tok_spec, pl.BlockSpec((ne, 1), lambda i: (0, 0))],
        out_specs=tok_spec,
        out_shape=jax.ShapeDtypeStruct((TOP_K, n), I32),
        compiler_params=_params(("arbitrary",)),
        name="moe_dest",
    )(idx, pos, start)


def _dispatch_kernel(dest_ref, h_ref, xs_ref, sem):
    tm = h_ref.shape[0]

    def issue(t, carry):
        for k in range(TOP_K):
            pltpu.make_async_copy(h_ref.at[pl.ds(t, 1), :], xs_ref.at[pl.ds(dest_ref[k, t], 1), :], sem).start()
        return carry

    lax.fori_loop(0, tm, issue, 0)
    for _ in range(TOP_K):
        pltpu.make_async_copy(h_ref, xs_ref.at[pl.ds(0, tm), :], sem).wait()


def _dispatch(h, dest):
    n, d = h.shape
    tm = TM_ROWS
    return pl.pallas_call(
        _dispatch_kernel,
        grid=(n // tm,),
        in_specs=[pl.BlockSpec((TOP_K, tm), lambda i: (0, i), memory_space=pltpu.SMEM),
                  pl.BlockSpec((tm, d), lambda i: (i, 0))],
        out_specs=pl.BlockSpec(memory_space=pl.ANY),
        out_shape=jax.ShapeDtypeStruct((n * TOP_K, d), F32),
        scratch_shapes=[pltpu.SemaphoreType.DMA(())],
        compiler_params=_params(("arbitrary",)),
        name="moe_dispatch",
    )(dest, h)


def _expert_kernel(tile_ref, exp_ref, lo_ref, hi_ref, newe_ref, newt_ref,
                   xs_ref, w1_ref, w3_ref, w2_ref, ys_ref, w1_s, w3_s, w2_s):
    i = pl.program_id(0)

    @pl.when(newe_ref[i] == 1)
    def _():
        w1_s[...] = w1_ref[0, 0].astype(BF16)
        w3_s[...] = w3_ref[0, 0].astype(BF16)
        w2_s[...] = w2_ref[0, 0].astype(BF16)

    tm = xs_ref.shape[0]
    lo, hi = lo_ref[i], hi_ref[i]

    @pl.when(hi > lo)
    def _():
        x = xs_ref[...].astype(BF16)
        a = jnp.dot(x, w1_s[...], preferred_element_type=F32)
        b = jnp.dot(x, w3_s[...], preferred_element_type=F32)
        hb = (a * jax.nn.sigmoid(a) * b).astype(BF16)
        y = jnp.dot(hb, w2_s[...], preferred_element_type=F32)
        row = tile_ref[i] * tm + lax.broadcasted_iota(I32, (tm, 1), 0)
        y = jnp.where((row >= lo) & (row < hi), y, 0.0)

        @pl.when(newt_ref[i] == 1)
        def _():
            ys_ref[...] = y

        @pl.when(newt_ref[i] == 0)
        def _():
            ys_ref[...] = ys_ref[...] + y


def _work_items(counts, n_rows, tm):
    ne = counts.shape[0]
    n_tiles = n_rows // tm
    n_items = n_tiles + ne - 1
    end = jnp.cumsum(counts)
    start = end - counts
    first_t = start // tm
    n_e = jnp.where(counts > 0, (end - 1) // tm - first_t + 1, 0)
    item_end = jnp.cumsum(n_e)
    item_off = item_end - n_e
    it = jnp.arange(n_items, dtype=I32)
    total = item_end[-1]
    valid = it < total
    e = jnp.minimum(jnp.searchsorted(item_end, jnp.minimum(it, total - 1), side='right'), ne - 1).astype(I32)
    tile = jnp.where(valid, first_t[e] + it - item_off[e], n_tiles - 1).astype(I32)
    lo = jnp.where(valid, jnp.maximum(start[e], tile * tm), 0).astype(I32)
    hi = jnp.where(valid, jnp.minimum(end[e], (tile + 1) * tm), 0).astype(I32)
    prev_e = jnp.concatenate([jnp.full((1,), -1, I32), e[:-1]])
    prev_t = jnp.concatenate([jnp.full((1,), -1, I32), tile[:-1]])
    return tile, e, lo, hi, (e != prev_e).astype(I32), (tile != prev_t).astype(I32)


def _experts(xs, items, w1, w3, w2, layer):
    n_rows, d = xs.shape
    de = w1.shape[3]
    tm = TM_EXPERT
    n_items = items[0].shape[0]
    grid_spec = pltpu.PrefetchScalarGridSpec(
        num_scalar_prefetch=6,
        grid=(n_items,),
        in_specs=[pl.BlockSpec((tm, d), lambda i, t, e, *_: (t[i], 0)),
                  pl.BlockSpec((1, 1, d, de), lambda i, t, e, *_: (layer, e[i], 0, 0)),
                  pl.BlockSpec((1, 1, d, de), lambda i, t, e, *_: (layer, e[i], 0, 0)),
                  pl.BlockSpec((1, 1, de, d), lambda i, t, e, *_: (layer, e[i], 0, 0))],
        out_specs=pl.BlockSpec((tm, d), lambda i, t, e, *_: (t[i], 0)),
        scratch_shapes=[pltpu.VMEM((d, de), BF16), pltpu.VMEM((d, de), BF16), pltpu.VMEM((de, d), BF16)],
    )
    return pl.pallas_call(
        _expert_kernel,
        grid_spec=grid_spec,
        out_shape=jax.ShapeDtypeStruct((n_rows, d), F32),
        compiler_params=_params(("arbitrary",)),
        name="moe_experts",
    )(*items, xs, w1, w3, w2)


def _combine_kernel(dest_ref, x_ref, h_ref, w_ref, mod_ref, s1_ref, s3_ref, s2_ref, fg_ref, ys_ref, o_ref,
                    buf, sem, *, final):
    tm = x_ref.shape[0]

    def issue(t, carry):
        for k in range(TOP_K):
            pltpu.make_async_copy(ys_ref.at[pl.ds(dest_ref[k, t], 1), :], buf.at[k, pl.ds(t, 1), :], sem).start()
        return carry

    lax.fori_loop(0, tm, issue, 0)
    hb = h_ref[...].astype(BF16)
    a = jnp.dot(hb, s1_ref[...], preferred_element_type=F32)
    b = jnp.dot(hb, s3_ref[...], preferred_element_type=F32)
    acc = jnp.dot((a * jax.nn.sigmoid(a) * b).astype(BF16), s2_ref[...], preferred_element_type=F32)
    for k in range(TOP_K):
        pltpu.make_async_copy(ys_ref.at[pl.ds(0, tm), :], buf.at[k], sem).wait()
    w = w_ref[...]
    for k in range(TOP_K):
        acc = acc + w[:, k:k + 1] * buf[k]
    m = mod_ref[0]
    y = x_ref[...] + m[5:6] * acc
    if final:
        y = y * lax.rsqrt(jnp.mean(y * y, axis=-1, keepdims=True) + EPS) * fg_ref[...]
    o_ref[...] = y


def _combine(x, h, ys, dest, w, mod, s1, s3, s2, final_g, cmap, final):
    n, d = x.shape
    ds_ = s1.shape[1]
    tm = TM_ROWS
    return pl.pallas_call(
        functools.partial(_combine_kernel, final=final),
        grid=(n // tm,),
        in_specs=[pl.BlockSpec((TOP_K, tm), lambda i: (0, i), memory_space=pltpu.SMEM),
                  pl.BlockSpec((tm, d), lambda i: (i, 0)),
                  pl.BlockSpec((tm, d), lambda i: (i, 0)),
                  pl.BlockSpec((tm, TOP_K), lambda i: (i, 0)),
                  pl.BlockSpec((1, N_MOD, d), lambda i: (cmap(i), 0, 0)),
                  pl.BlockSpec((d, ds_), lambda i: (0, 0)),
                  pl.BlockSpec((d, ds_), lambda i: (0, 0)),
                  pl.BlockSpec((ds_, d), lambda i: (0, 0)),
                  pl.BlockSpec((1, d), lambda i: (0, 0)),
                  pl.BlockSpec(memory_space=pl.ANY)],
        out_specs=pl.BlockSpec((tm, d), lambda i: (i, 0)),
        out_shape=jax.ShapeDtypeStruct((n, d), F32),
        scratch_shapes=[pltpu.VMEM((TOP_K, tm, d), F32), pltpu.SemaphoreType.DMA(())],
        compiler_params=_params(("arbitrary",)),
        name="moe_combine",
    )(dest, x, h, w, mod, s1, s3, s2, final_g.reshape(1, d), ys)


def _moe(x, mod, layer, cmaps, p, final):
    cmap_r, cmap_c = cmaps
    n, d = x.shape
    rw = p['moe_router_w'][layer].T
    rw_hi = rw.astype(BF16)
    rw_lo = (rw - rw_hi.astype(F32)).astype(BF16)
    h, idx, wts, pos, cnt = _router(x, mod, p['norm2_g'][layer], rw_hi, rw_lo, p['moe_router_b'][layer], cmap_r)
    counts = cnt[:, 0].astype(I32)
    start = jnp.cumsum(counts) - counts
    dest = _dest(idx, pos, start.astype(F32).reshape(-1, 1))
    xs = _dispatch(h, dest)
    items = _work_items(counts, n * TOP_K, TM_EXPERT)
    ys = _experts(xs, items, p['moe_w1'], p['moe_w3'], p['moe_w2'], layer)
    return _combine(x, h, ys, dest, wts.T, mod,
                    p['moe_shared_w1'][layer].astype(BF16), p['moe_shared_w3'][layer].astype(BF16),
                    p['moe_shared_w2'][layer].astype(BF16), p['final_g'], cmap_c, final)


def kernel(x_prompt, x_sample, c, state_ssm, cache_k, cache_v, c_ctx, norm1_g, norm2_g, ada_w, ada_b, ssm_lam_re, ssm_lam_im, ssm_log_dt, ssm_b_re, ssm_b_im, ssm_c_re, ssm_c_im, ssm_d, ssm_glu_w, attn_w_qkv, attn_lam_q, attn_lam_k, attn_subln_g, attn_w_o, moe_router_w, moe_router_b, moe_w1, moe_w3, moe_w2, moe_shared_w1, moe_shared_w3, moe_shared_w2, final_g):
    p = dict(norm2_g=norm2_g, moe_router_w=moe_router_w, moe_router_b=moe_router_b,
             moe_w1=moe_w1, moe_w3=moe_w3, moe_w2=moe_w2, moe_shared_w1=moe_shared_w1,
             moe_shared_w3=moe_shared_w3, moe_shared_w2=moe_shared_w2, final_g=final_g)
    b_ctx, l_ctx, d = x_prompt.shape
    b_lat, l_lat, _ = x_sample.shape
    n_ctx, n_lat = b_ctx * l_ctx, b_lat * l_lat
    g = d // SSM_GROUP
    pdim = ssm_lam_re.shape[-1]
    nh = d // V_DIM
    assert 2 * pdim == LANES and CHUNK * SSM_GROUP == 2 * LANES
    assert n_ctx % TM_DENSE == 0 and l_lat % TM_DENSE == 0 and b_lat % 8 == 0 and b_ctx % 8 == 0
    cmap = _cond_map(TM_DENSE, n_ctx, l_lat)
    cmap_r = _cond_map(TM_ROUTER, n_ctx, l_lat)
    cmap_c = _cond_map(TM_ROWS, n_ctx, l_lat)

    nc = 1 + b_lat
    nc_pad = -(-nc // 8) * 8
    cond = jnp.concatenate([c_ctx[None, :], c, jnp.zeros((nc_pad - nc, d), F32)], axis=0)
    mod = _ada(cond, ada_w, ada_b)

    x = jnp.concatenate([x_prompt.reshape(n_ctx, d), x_sample.reshape(n_lat, d)], axis=0)

    h = _norm1(x, mod[0], norm1_g[0], cmap)
    wts = _s5_weights(ssm_lam_re[0], ssm_lam_im[0], ssm_log_dt[0], ssm_b_re[0], ssm_b_im[0],
                      ssm_c_re[0], ssm_c_im[0])
    y_ctx, fin = _s5(h[:n_ctx].reshape(b_ctx, l_ctx, d), jnp.zeros((g, b_ctx, 2 * LANES), F32), wts)
    h0 = state_ssm[:, 0].transpose(3, 0, 2, 1, 4).reshape(g, b_lat, 2 * LANES)
    y_lat, _ = _s5(h[n_ctx:].reshape(b_lat, l_lat, d), h0, wts)
    new_state = fin.reshape(g, b_ctx, 2, 2, pdim).transpose(1, 3, 2, 0, 4)[:, None]
    x = _glu(x, jnp.concatenate([y_ctx, y_lat], axis=0), mod[0], norm1_g[0], ssm_d[0],
             ssm_glu_w[0].astype(BF16), cmap)
    x = _moe(x, mod[0], 0, (cmap_r, cmap_c), p, final=False)

    lam_init = 0.8 - 0.6 * math.exp(-0.3 * 1)
    wqkv = attn_w_qkv[0].astype(BF16)
    q_c, k_c, v_c, kf, vf = _qkv(x, mod[1], norm1_g[1], wqkv, cmap, 0, n_ctx)
    q_l, k_l, v_l = _qkv(x, mod[1], norm1_g[1], wqkv, cmap, n_ctx, n_lat, rope_l=l_lat)
    new_k = kf.reshape(b_ctx, 1, l_ctx, nh, 2, HEAD_DIM)
    new_v = vf.reshape(b_ctx, 1, l_ctx, nh, V_DIM)
    o_c = _attention(q_c.reshape(b_ctx, l_ctx, d), k_c.reshape(b_ctx, l_ctx, d), v_c.reshape(b_ctx, l_ctx, d),
                     attn_lam_q[0], attn_lam_k[0], attn_subln_g[0], lam_init)
    past = cache_k.shape[2]
    k_all = jnp.concatenate([k_l.reshape(b_lat, l_lat, d), cache_k[:, 0].reshape(b_lat, past, d).astype(BF16)], axis=1)
    v_all = jnp.concatenate([v_l.reshape(b_lat, l_lat, d), cache_v[:, 0].reshape(b_lat, past, d).astype(BF16)], axis=1)
    o_l = _attention(q_l.reshape(b_lat, l_lat, d), k_all, v_all,
                     attn_lam_q[0], attn_lam_k[0], attn_subln_g[0], lam_init)
    o = jnp.concatenate([o_c.reshape(n_ctx, d), o_l.reshape(n_lat, d)], axis=0)
    x = _oproj(x, o, mod[1], attn_w_o[0].astype(BF16), cmap)
    y = _moe(x, mod[1], 1, (cmap_r, cmap_c), p, final=True)

    return (y[:n_ctx].reshape(b_ctx, l_ctx, d), y[n_ctx:].reshape(b_lat, l_lat, d),
            new_state, new_k, new_v)
```

```python
import functools
import math

import jax
import jax.numpy as jnp
from jax import lax
from jax.experimental import pallas as pl
from jax.experimental.pallas import tpu as pltpu

F32 = jnp.float32
BF16 = jnp.bfloat16
I32 = jnp.int32
U32 = jnp.uint32

EPS = 1e-6
SSM_GROUP = 16
CHUNK = 16
GRID_W = 64
HEAD_DIM = 64
V_DIM = 2 * HEAD_DIM
ROPE_THETA = 10000.0
TOP_K = 8
N_EXPERT_GROUPS = 8
TOPK_GROUPS = 4
ROUTED_SCALE = 2.5
N_MOD = 6

LANES = 128
VMEM_LIMIT = 48 * 1024 * 1024

TM_DENSE = 512
TM_ROUTER = 256
TM_EXPERT = 512
TM_ROWS = 128
TQ_ATTN = 256


def _params(sem):
    return pltpu.CompilerParams(dimension_semantics=sem, vmem_limit_bytes=VMEM_LIMIT)


def _normmod(x, g, sc, sh):
    ms = jnp.mean(x * x, axis=-1, keepdims=True)
    return (x * lax.rsqrt(ms + EPS) * g) * (1.0 + sc) + sh


def _cond_map(tm, n_lat, dec_seq):
    t_lat = n_lat // tm
    per = dec_seq // tm

    def f(i):
        return jnp.where(i < t_lat, 1 + i // per, 0)
    return f


def _ada_kernel(c_ref, w_ref, b_ref, o_ref):
    c = c_ref[...]
    s = (c * jax.nn.sigmoid(c)).astype(BF16)
    o_ref[0] = jnp.dot(s, w_ref[0].astype(BF16), preferred_element_type=F32) + b_ref[0]


def _ada(cond, ada_w, ada_b):
    depth, d, n6 = ada_w.shape
    nc = cond.shape[0]
    tn = 1536
    out = pl.pallas_call(
        _ada_kernel,
        grid=(depth, n6 // tn),
        in_specs=[pl.BlockSpec((nc, d), lambda l, j: (0, 0)),
                  pl.BlockSpec((1, d, tn), lambda l, j: (l, 0, j)),
                  pl.BlockSpec((1, 1, tn), lambda l, j: (l, 0, j))],
        out_specs=pl.BlockSpec((1, nc, tn), lambda l, j: (l, 0, j)),
        out_shape=jax.ShapeDtypeStruct((depth, nc, n6), F32),
        compiler_params=_params(("arbitrary", "arbitrary")),
        name="ada",
    )(cond, ada_w, ada_b.reshape(depth, 1, n6))
    return out.reshape(depth, nc, N_MOD, d)


GROUPS_PER_SLAB = LANES // SSM_GROUP
HALF_CHUNK = CHUNK // 2


def _chunk_perm():
    r = jnp.arange(HALF_CHUNK * LANES)
    t8, g8, c = r // LANES, (r // SSM_GROUP) % GROUPS_PER_SLAB, r % SSM_GROUP
    col = g8 * LANES + t8 * SSM_GROUP + c
    return (col[:, None] == r[None, :]).astype(BF16)


def _pack_kernel(x_ref, mod_ref, g_ref, p_ref, o_ref, h_s):
    m = mod_ref[0]
    h = _normmod(x_ref[...], g_ref[...], m[1:2], m[0:1])
    tm, d = h.shape
    nct = tm // CHUNK
    for j in range(d // LANES):
        h_s[j] = h[:, j * LANES:(j + 1) * LANES]
    slabs = []
    for j in range(d // LANES):
        for hf in range(2):
            slabs.append(jnp.concatenate(
                [h_s[j, pl.ds(hf * HALF_CHUNK + t8, nct, stride=CHUNK), :] for t8 in range(HALF_CHUNK)], axis=1))
    lhs = jnp.concatenate(slabs, axis=0).astype(BF16)
    out = jnp.dot(lhs, p_ref[...], preferred_element_type=F32)
    kw = CHUNK * SSM_GROUP
    for j in range(d // LANES):
        for hf in range(2):
            blk = out[(2 * j + hf) * nct:(2 * j + hf + 1) * nct]
            for g8 in range(GROUPS_PER_SLAB):
                col = (j * GROUPS_PER_SLAB + g8) * kw + hf * LANES
                o_ref[:, col:col + LANES] = blk[:, g8 * LANES:(g8 + 1) * LANES]


def _s5_pack(x, mod, g, cmap):
    n, d = x.shape
    tm = TM_DENSE
    return pl.pallas_call(
        _pack_kernel,
        grid=(n // tm,),
        in_specs=[pl.BlockSpec((tm, d), lambda i: (i, 0)),
                  pl.BlockSpec((1, N_MOD, d), lambda i: (cmap(i), 0, 0)),
                  pl.BlockSpec((1, d), lambda i: (0, 0)),
                  pl.BlockSpec((HALF_CHUNK * LANES, HALF_CHUNK * LANES), lambda i: (0, 0))],
        out_specs=pl.BlockSpec((tm // CHUNK, CHUNK * d), lambda i: (i, 0)),
        out_shape=jax.ShapeDtypeStruct((n // CHUNK, CHUNK * d), F32),
        scratch_shapes=[pltpu.VMEM((d // LANES, tm, LANES), F32)],
        compiler_params=_params(("arbitrary",)),
        name="s5_pack",
    )(x, mod, g.reshape(1, d), _chunk_perm())


def _s5_weights(lam_re, lam_im, log_dt, b_re, b_im, c_re, c_im):
    hi = lax.Precision.HIGHEST
    t = CHUNK
    w_in, w_out, toep, a16 = [], [], [], []
    for dr in range(2):
        dt = jnp.exp(log_dt[dr])[:, None]
        lr, li = lam_re[dr], lam_im[dr]
        mag = jnp.exp(lr * dt)
        a_re = mag * jnp.cos(li * dt)
        a_im = mag * jnp.sin(li * dt)
        den = lr * lr + li * li
        f_re = ((a_re - 1.0) * lr + a_im * li) / den
        f_im = (a_im * lr - (a_re - 1.0) * li) / den
        bb_re = f_re[..., None] * b_re[dr] - f_im[..., None] * b_im[dr]
        bb_im = f_re[..., None] * b_im[dr] + f_im[..., None] * b_re[dr]
        k = jnp.arange(t + 1, dtype=F32)[:, None, None]
        pm = jnp.exp(k * (lr * dt)[None])
        pk_re = pm * jnp.cos(k * (li * dt)[None])
        pk_im = pm * jnp.sin(k * (li * dt)[None])
        ks = jnp.arange(t)
        sel = (t - 1 - ks) if dr == 0 else ks
        pr, pi = pk_re[sel], pk_im[sel]
        wi_re = pr[..., None] * bb_re[None] - pi[..., None] * bb_im[None]
        wi_im = pr[..., None] * bb_im[None] + pi[..., None] * bb_re[None]
        g_, p_, c_ = bb_re.shape
        w_in.append((wi_re.transpose(1, 0, 3, 2).reshape(g_, t * c_, p_),
                     wi_im.transpose(1, 0, 3, 2).reshape(g_, t * c_, p_)))
        sel = (ks + 1) if dr == 0 else (t - ks)
        pr, pi = pk_re[sel], pk_im[sel]
        ca_re = c_re[dr][None] * pr[:, :, None, :] - c_im[dr][None] * pi[:, :, None, :]
        ca_im = c_re[dr][None] * pi[:, :, None, :] + c_im[dr][None] * pr[:, :, None, :]
        w_out.append((ca_re.transpose(1, 3, 0, 2).reshape(g_, p_, t * c_),
                      (-ca_im).transpose(1, 3, 0, 2).reshape(g_, p_, t * c_)))
        pr, pi = pk_re[:t], pk_im[:t]
        cd_re = c_re[dr][None] * pr[:, :, None, :] - c_im[dr][None] * pi[:, :, None, :]
        cd_im = c_re[dr][None] * pi[:, :, None, :] + c_im[dr][None] * pr[:, :, None, :]
        kk = (jnp.einsum('kgop,gpi->kgoi', cd_re, bb_re, precision=hi)
              - jnp.einsum('kgop,gpi->kgoi', cd_im, bb_im, precision=hi))
        s_i = ks[:, None]
        t_i = ks[None, :]
        delta = (t_i - s_i) if dr == 0 else (s_i - t_i)
        kd = jnp.where((delta >= 0)[:, :, None, None, None], kk[jnp.clip(delta, 0, t - 1)], 0.0)
        toep.append(kd.transpose(2, 0, 4, 1, 3).reshape(g_, t * c_, t * c_))
        a16.append((pk_re[t], pk_im[t]))
    w_in_all = jnp.concatenate([w_in[0][0], w_in[1][0], w_in[0][1], w_in[1][1]], axis=2)
    w_out_all = jnp.concatenate([w_out[0][0], w_out[1][0], w_out[0][1], w_out[1][1]], axis=1)
    toep_all = toep[0] + toep[1]
    a16_all = jnp.stack([jnp.concatenate([a16[0][0], a16[1][0]], axis=1),
                         jnp.concatenate([a16[0][1], a16[1][1]], axis=1)], axis=1)
    return w_in_all.astype(BF16), toep_all.astype(BF16), w_out_all.astype(BF16), a16_all


def _s5_kernel(u_ref, win_ref, wtp_ref, wout_ref, a16_ref, h0_ref, y_ref, fin_ref, v_s, sf_s, sb_s, *, nc, nb):
    u = u_ref[...].astype(BF16)
    v = jnp.dot(u, win_ref[0], preferred_element_type=F32)
    v_s[0] = v[:, :LANES]
    v_s[1] = v[:, LANES:]
    a = a16_ref[0]
    ar, ai = a[0:1], a[1:2]
    fwd = lax.broadcasted_iota(I32, (nb, LANES), 1) < (LANES // 2)

    def step(i, carry):
        sr, si = carry
        rf = pl.ds(i, nb, stride=nc)
        rb = pl.ds(nc - 1 - i, nb, stride=nc)
        sf_s[0, rf, :] = sr
        sf_s[1, rf, :] = si
        sb_s[0, rb, :] = sr
        sb_s[1, rb, :] = si
        vr = jnp.where(fwd, v_s[0, rf, :], v_s[0, rb, :])
        vi = jnp.where(fwd, v_s[1, rf, :], v_s[1, rb, :])
        return ar * sr - ai * si + vr, ar * si + ai * sr + vi

    h0 = h0_ref[0]
    sr, si = lax.fori_loop(0, nc, step, (h0[:, :LANES], h0[:, LANES:]))
    fin_ref[0] = jnp.concatenate([sr, si], axis=1)
    fwd_all = lax.broadcasted_iota(I32, (nc * nb, LANES), 1) < (LANES // 2)
    hs = jnp.concatenate([jnp.where(fwd_all, sf_s[0], sb_s[0]), jnp.where(fwd_all, sf_s[1], sb_s[1])],
                         axis=1).astype(BF16)
    y_ref[...] = (jnp.dot(u, wtp_ref[0], preferred_element_type=F32)
                  + jnp.dot(hs, wout_ref[0], preferred_element_type=F32))


def _s5(u, h0, wts, row_blk, nb, nc):
    w_in, w_tp, w_out, a16 = wts
    r = nb * nc
    kw = CHUNK * SSM_GROUP
    g = u.shape[1] // kw
    return pl.pallas_call(
        functools.partial(_s5_kernel, nc=nc, nb=nb),
        grid=(g,),
        in_specs=[pl.BlockSpec((r, kw), lambda j: (row_blk, j)),
                  pl.BlockSpec((1, kw, kw), lambda j: (j, 0, 0)),
                  pl.BlockSpec((1, kw, kw), lambda j: (j, 0, 0)),
                  pl.BlockSpec((1, kw, kw), lambda j: (j, 0, 0)),
                  pl.BlockSpec((1, 2, LANES), lambda j: (j, 0, 0)),
                  pl.BlockSpec((1, nb, kw), lambda j: (j, 0, 0))],
        out_specs=[pl.BlockSpec((r, kw), lambda j: (0, j)),
                   pl.BlockSpec((1, nb, kw), lambda j: (j, 0, 0))],
        out_shape=[jax.ShapeDtypeStruct((r, u.shape[1]), F32),
                   jax.ShapeDtypeStruct((g, nb, kw), F32)],
        scratch_shapes=[pltpu.VMEM((2, r, LANES), F32)] * 3,
        compiler_params=_params(("arbitrary",)),
        name="s5_scan",
    )(u, w_in, w_tp, w_out, a16, h0)


def _glu_kernel(x_ref, ya_ref, yb_ref, mod_ref, g_ref, d_ref, pt_ref, w_ref, o_ref, y_s, *, tiles_a):
    tm, dd = x_ref.shape
    nct = tm // CHUNK
    kw = CHUNK * SSM_GROUP
    from_a = pl.program_id(0) < tiles_a
    slabs = []
    for j in range(dd // LANES):
        for hf in range(2):
            cols = [(j * GROUPS_PER_SLAB + g8) * kw + hf * LANES for g8 in range(GROUPS_PER_SLAB)]
            slabs.append(jnp.concatenate(
                [jnp.where(from_a, ya_ref[:, c0:c0 + LANES], yb_ref[:, c0:c0 + LANES]) for c0 in cols], axis=1))
    lhs = jnp.concatenate(slabs, axis=0)
    hi = lhs.astype(BF16)
    lo = (lhs - hi.astype(F32)).astype(BF16)
    pt = pt_ref[...]
    nat = jnp.dot(hi, pt, preferred_element_type=F32) + jnp.dot(lo, pt, preferred_element_type=F32)
    for j in range(dd // LANES):
        for hf in range(2):
            blk = nat[(2 * j + hf) * nct:(2 * j + hf + 1) * nct]
            for t8 in range(HALF_CHUNK):
                y_s[j, pl.ds(hf * HALF_CHUNK + t8, nct, stride=CHUNK), :] = blk[:, t8 * LANES:(t8 + 1) * LANES]
    m = mod_ref[0]
    x = x_ref[...]
    h = _normmod(x, g_ref[...], m[1:2], m[0:1])
    y = d_ref[...] * h + jnp.concatenate([y_s[j] for j in range(dd // LANES)], axis=1)
    gl = jax.nn.gelu(y).astype(BF16)
    z = jnp.dot(gl, w_ref[...], preferred_element_type=F32)
    o_ref[...] = x + m[2:3] * (z[:, :dd] * jax.nn.sigmoid(z[:, dd:]))


def _glu(x, y_a, y_b, mod, g, dskip, w, cmap):
    n, d = x.shape
    tm = TM_DENSE
    nct = tm // CHUNK
    tiles_a = y_a.shape[0] // nct
    last_a = tiles_a - 1
    return pl.pallas_call(
        functools.partial(_glu_kernel, tiles_a=tiles_a),
        grid=(n // tm,),
        in_specs=[pl.BlockSpec((tm, d), lambda i: (i, 0)),
                  pl.BlockSpec((nct, CHUNK * d), lambda i: (jnp.minimum(i, last_a), 0)),
                  pl.BlockSpec((nct, CHUNK * d), lambda i: (jnp.maximum(i - tiles_a, 0), 0)),
                  pl.BlockSpec((1, N_MOD, d), lambda i: (cmap(i), 0, 0)),
                  pl.BlockSpec((1, d), lambda i: (0, 0)),
                  pl.BlockSpec((1, d), lambda i: (0, 0)),
                  pl.BlockSpec((HALF_CHUNK * LANES, HALF_CHUNK * LANES), lambda i: (0, 0)),
                  pl.BlockSpec((d, 2 * d), lambda i: (0, 0))],
        out_specs=pl.BlockSpec((tm, d), lambda i: (i, 0)),
        out_shape=jax.ShapeDtypeStruct((n, d), F32),
        scratch_shapes=[pltpu.VMEM((d // LANES, tm, LANES), F32)],
        compiler_params=_params(("arbitrary",)),
        name="s5_glu",
    )(x, y_a, y_b, mod, g.reshape(1, d), dskip.reshape(1, d), _chunk_perm().T, w)


def _rope_tables(l):
    r = HEAD_DIM // 2
    half = r // 2
    freq = ROPE_THETA ** (-jnp.arange(half, dtype=F32) / half)
    pos = jnp.arange(l)
    row = (pos // GRID_W).astype(F32)
    col = (pos % GRID_W).astype(F32)
    d = jnp.arange(LANES) % HEAD_DIM
    p = jnp.where((d < r)[None, :], row[:, None], col[:, None])
    ang = p * freq[d % half][None, :]
    sign = jnp.where((d % r) < half, -1.0, 1.0)[None, :]
    return jnp.cos(ang), jnp.sin(ang) * sign


def _qkv_kernel(*refs, rope):
    if rope:
        x_ref, mod_ref, g_ref, w_ref, cos_ref, sin_ref, q_ref, k_ref, v_ref = refs
    else:
        x_ref, mod_ref, g_ref, w_ref, q_ref, k_ref, v_ref, kf_ref, vf_ref = refs
    m = mod_ref[0]
    h = _normmod(x_ref[...], g_ref[...], m[1:2], m[0:1]).astype(BF16)
    qkv = jnp.dot(h, w_ref[...], preferred_element_type=F32)
    d = x_ref.shape[1]
    q = qkv[:, :d] * (HEAD_DIM ** -0.5)
    k = qkv[:, d:2 * d]
    v = qkv[:, 2 * d:]
    v_ref[...] = v.astype(BF16)
    if not rope:
        q_ref[...] = q.astype(BF16)
        k_ref[...] = k.astype(BF16)
        kf_ref[...] = k
        vf_ref[...] = v
        return
    cos = cos_ref[...]
    sin = sin_ref[...]
    half = HEAD_DIM // 4
    lane = lax.broadcasted_iota(I32, cos.shape, 1)
    up_sel = (lane % (2 * half)) < half
    for src, dst in ((q, q_ref), (k, k_ref)):
        for j in range(d // LANES):
            blk = src[:, j * LANES:(j + 1) * LANES]
            up = pltpu.roll(blk, LANES - half, 1)
            dn = pltpu.roll(blk, half, 1)
            dst[:, j * LANES:(j + 1) * LANES] = (blk * cos + jnp.where(up_sel, up, dn) * sin).astype(BF16)


def _qkv(x, mod, g, w, cmap, row0, nrows, rope_l=None):
    n, d = x.shape
    tm = TM_DENSE
    t0 = row0 // tm
    rope = rope_l is not None
    in_specs = [pl.BlockSpec((tm, d), lambda i: (i + t0, 0)),
                pl.BlockSpec((1, N_MOD, d), lambda i: (cmap(i + t0), 0, 0)),
                pl.BlockSpec((1, d), lambda i: (0, 0)),
                pl.BlockSpec((d, 3 * d), lambda i: (0, 0))]
    args = [x, mod, g.reshape(1, d), w]
    row_spec = pl.BlockSpec((tm, d), lambda i: (i, 0))
    out_specs = [row_spec] * 3
    out_shape = [jax.ShapeDtypeStruct((nrows, d), BF16)] * 3
    if rope:
        per = rope_l // tm
        cos, sin = _rope_tables(rope_l)
        in_specs += [pl.BlockSpec((tm, LANES), lambda i: (i % per, 0))] * 2
        args += [cos, sin]
    else:
        out_specs += [row_spec] * 2
        out_shape += [jax.ShapeDtypeStruct((nrows, d), F32)] * 2
    return pl.pallas_call(
        functools.partial(_qkv_kernel, rope=rope),
        grid=(nrows // tm,),
        in_specs=in_specs,
        out_specs=out_specs,
        out_shape=out_shape,
        compiler_params=_params(("arbitrary",)),
        name="qkv_rope" if rope else "qkv",
    )(*args)


def _attn_kernel(q_ref, kt_ref, v_ref, lq_ref, lk_ref, g_ref, o_ref, *, lam_init):
    q = q_ref[0]
    tq = q.shape[0]
    lane = lax.broadcasted_iota(I32, q.shape, 1)
    zero = jnp.zeros_like(q)
    qq = jnp.concatenate([jnp.where(lane < HEAD_DIM, q, zero), jnp.where(lane >= HEAD_DIM, q, zero)], axis=0)
    s = jnp.dot(qq, kt_ref[0, 0], preferred_element_type=F32)
    p = jnp.exp(s - jnp.max(s, axis=1, keepdims=True))
    l = jnp.sum(p, axis=1, keepdims=True)
    lqk = lq_ref[...] * lk_ref[...]
    lam = (jnp.exp(jnp.sum(lqk[0:1], axis=1, keepdims=True))
           - jnp.exp(jnp.sum(lqk[1:2], axis=1, keepdims=True)) + lam_init)
    w = p[:tq] * (1.0 / l[:tq]) - p[tq:] * (lam / l[tq:])
    o = jnp.dot(w.astype(BF16), v_ref[0], preferred_element_type=F32)
    o = o * lax.rsqrt(jnp.mean(o * o, axis=1, keepdims=True) + EPS) * g_ref[...]
    o_ref[0] = (o * (1.0 - lam_init)).astype(o_ref.dtype)


def _attention(q, k, v, lam_q, lam_k, subln_g, lam_init):
    nb, l, d = q.shape
    lk = k.shape[1]
    nh = d // V_DIM
    kt = k.reshape(nb, lk, nh, V_DIM).transpose(0, 2, 3, 1)
    tq = min(TQ_ATTN, l)
    return pl.pallas_call(
        functools.partial(_attn_kernel, lam_init=lam_init),
        grid=(nb, nh, l // tq),
        in_specs=[pl.BlockSpec((1, tq, V_DIM), lambda b, h, i: (b, i, h)),
                  pl.BlockSpec((1, 1, V_DIM, lk), lambda b, h, i: (b, h, 0, 0)),
                  pl.BlockSpec((1, lk, V_DIM), lambda b, h, i: (b, 0, h)),
                  pl.BlockSpec((2, HEAD_DIM), lambda b, h, i: (0, 0)),
                  pl.BlockSpec((2, HEAD_DIM), lambda b, h, i: (0, 0)),
                  pl.BlockSpec((1, V_DIM), lambda b, h, i: (0, 0))],
        out_specs=pl.BlockSpec((1, tq, V_DIM), lambda b, h, i: (b, i, h)),
        out_shape=jax.ShapeDtypeStruct((nb, l, d), BF16),
        compiler_params=_params(("arbitrary", "arbitrary", "arbitrary")),
        name="diff_attn",
    )(q, kt, v, lam_q, lam_k, subln_g.reshape(1, V_DIM))


def _oproj_kernel(x_ref, o_ref, mod_ref, w_ref, y_ref):
    m = mod_ref[0]
    y_ref[...] = x_ref[...] + m[2:3] * jnp.dot(o_ref[...], w_ref[...], preferred_element_type=F32)


def _oproj(x, o, mod, w, cmap):
    n, d = x.shape
    tm = TM_DENSE
    return pl.pallas_call(
        _oproj_kernel,
        grid=(n // tm,),
        in_specs=[pl.BlockSpec((tm, d), lambda i: (i, 0)),
                  pl.BlockSpec((tm, d), lambda i: (i, 0)),
                  pl.BlockSpec((1, N_MOD, d), lambda i: (cmap(i), 0, 0)),
                  pl.BlockSpec((d, d), lambda i: (0, 0))],
        out_specs=pl.BlockSpec((tm, d), lambda i: (i, 0)),
        out_shape=jax.ShapeDtypeStruct((n, d), F32),
        compiler_params=_params(("arbitrary",)),
        name="attn_out",
    )(x, o, mod, w)


def _pack_rows(h):
    half = h.shape[1] // 2
    bits = lax.bitcast_convert_type(h.astype(BF16).astype(F32), U32)
    return bits[:, half:] | (bits[:, :half] >> 16)


def _unpack_rows(w):
    lo = lax.bitcast_convert_type(w << 16, F32)
    hi = lax.bitcast_convert_type(w & jnp.uint32(0xFFFF0000), F32)
    return lo, hi


def _stack_rows(rows, width):
    ri = lax.broadcasted_iota(I32, (len(rows), width), 0)
    out = jnp.zeros((len(rows), width), rows[0].dtype)
    for j, r in enumerate(rows):
        out = jnp.where(ri == j, r, out)
    return out


def _router_kernel(x_ref, mod_ref, g_ref, wh_ref, wl_ref, b_ref, h_ref, idx_ref, w_ref, pos_ref, cnt_ref, cnt_s):
    step = pl.program_id(0)

    @pl.when(step == 0)
    def _():
        cnt_s[...] = jnp.zeros_like(cnt_s)

    m = mod_ref[0]
    h = _normmod(x_ref[...], g_ref[...], m[4:5], m[3:4])
    h_ref[...] = _pack_rows(h)
    tr = h.shape[0]
    ne = wh_ref.shape[0]
    per = ne // N_EXPERT_GROUPS
    hh = h.astype(BF16)
    hl = (h - hh.astype(F32)).astype(BF16)
    nt = (((1,), (1,)), ((), ()))
    wh = wh_ref[...]
    logits = (lax.dot_general(wh, hh, nt, preferred_element_type=F32)
              + lax.dot_general(wl_ref[...], hh, nt, preferred_element_type=F32)
              + lax.dot_general(wh, hl, nt, preferred_element_type=F32))
    scores = jax.nn.sigmoid(logits)
    biased = scores + b_ref[...]
    neg = -jnp.inf
    ri = lax.broadcasted_iota(I32, (per, tr), 0).astype(F32)
    blocks, gscores = [], []
    for j in range(N_EXPERT_GROUPS):
        blk = biased[j * per:(j + 1) * per, :]
        m1 = jnp.max(blk, axis=0, keepdims=True)
        f1 = jnp.min(jnp.where(blk == m1, ri, float(per)), axis=0, keepdims=True)
        m2 = jnp.max(jnp.where(ri == f1, neg, blk), axis=0, keepdims=True)
        blocks.append(blk)
        gscores.append(m1 + m2)
    gs = _stack_rows(gscores, tr)
    gi = lax.broadcasted_iota(I32, gs.shape, 0).astype(F32)
    gsel = jnp.zeros(gs.shape, F32)
    for _ in range(TOPK_GROUPS):
        mm = jnp.max(gs, axis=0, keepdims=True)
        pick = gi == jnp.min(jnp.where(gs == mm, gi, float(N_EXPERT_GROUPS)), axis=0, keepdims=True)
        gsel = jnp.where(pick, 1.0, gsel)
        gs = jnp.where(pick, neg, gs)
    masked = jnp.concatenate(
        [jnp.where(gsel[j:j + 1, :] > 0.0, blocks[j], neg) for j in range(N_EXPERT_GROUPS)], axis=0)
    ei = lax.broadcasted_iota(I32, masked.shape, 0).astype(F32)
    picks, idxs, ws = [], [], []
    for _ in range(TOP_K):
        mm = jnp.max(masked, axis=0, keepdims=True)
        ff = jnp.min(jnp.where(masked == mm, ei, float(ne)), axis=0, keepdims=True)
        pick = ei == ff
        picks.append(pick)
        idxs.append(ff)
        ws.append(jnp.sum(jnp.where(pick, scores, 0.0), axis=0, keepdims=True))
        masked = jnp.where(pick, neg, masked)
    wsum = ws[0]
    for wk in ws[1:]:
        wsum = wsum + wk
    w_ref[...] = _stack_rows([wk / wsum * ROUTED_SCALE for wk in ws], tr)
    idx_ref[...] = _stack_rows(idxs, tr).astype(I32)
    sel = jnp.zeros(masked.shape, F32)
    for pick in picks:
        sel = sel + jnp.where(pick, 1.0, 0.0)
    earlier = (lax.broadcasted_iota(I32, (tr, tr), 0) < lax.broadcasted_iota(I32, (tr, tr), 1))
    prefix = jnp.dot(sel.astype(BF16), jnp.where(earlier, 1.0, 0.0).astype(BF16), preferred_element_type=F32)
    base = prefix + cnt_s[...]
    pos_ref[...] = _stack_rows(
        [jnp.sum(jnp.where(pick, base, 0.0), axis=0, keepdims=True) for pick in picks], tr).astype(I32)
    cnt_s[...] = cnt_s[...] + jnp.sum(sel, axis=1, keepdims=True)
    cnt_ref[...] = cnt_s[...]


def _router(x, mod, g, w_t_hi, w_t_lo, bias, cmap):
    n, d = x.shape
    ne = w_t_hi.shape[0]
    tr = TM_ROUTER
    tok_spec = pl.BlockSpec((TOP_K, tr), lambda i: (0, i))
    return pl.pallas_call(
        _router_kernel,
        grid=(n // tr,),
        in_specs=[pl.BlockSpec((tr, d), lambda i: (i, 0)),
                  pl.BlockSpec((1, N_MOD, d), lambda i: (cmap(i), 0, 0)),
                  pl.BlockSpec((1, d), lambda i: (0, 0)),
                  pl.BlockSpec((ne, d), lambda i: (0, 0)),
                  pl.BlockSpec((ne, d), lambda i: (0, 0)),
                  pl.BlockSpec((ne, 1), lambda i: (0, 0))],
        out_specs=[pl.BlockSpec((tr, d // 2), lambda i: (i, 0)), tok_spec, tok_spec, tok_spec,
                   pl.BlockSpec((ne, 1), lambda i: (0, 0))],
        out_shape=[jax.ShapeDtypeStruct((n, d // 2), U32),
                   jax.ShapeDtypeStruct((TOP_K, n), I32),
                   jax.ShapeDtypeStruct((TOP_K, n), F32),
                   jax.ShapeDtypeStruct((TOP_K, n), I32),
                   jax.ShapeDtypeStruct((ne, 1), F32)],
        scratch_shapes=[pltpu.VMEM((ne, 1), F32)],
        compiler_params=_params(("arbitrary",)),
        name="moe_router",
    )(x, mod, g.reshape(1, d), w_t_hi, w_t_lo, bias.reshape(ne, 1))


def _dest_kernel(idx_ref, pos_ref, start_ref, o_ref):
    idx = idx_ref[...]
    ei = lax.broadcasted_iota(I32, (start_ref.shape[0], idx.shape[1]), 0)
    start = start_ref[...]
    rows = [jnp.sum(jnp.where(ei == idx[k:k + 1], start, 0.0), axis=0, keepdims=True) for k in range(TOP_K)]
    o_ref[...] = _stack_rows(rows, idx.shape[1]).astype(I32) + pos_ref[...]


def _dest(idx, pos, start):
    n = idx.shape[1]
    ne = start.shape[0]
    tr = 512
    tok_spec = pl.BlockSpec((TOP_K, tr), lambda i: (0, i))
    return pl.pallas_call(
        _dest_kernel,
        grid=(n // tr,),
        in_specs=[tok_spec, tok_spec, pl.BlockSpec((ne, 1), lambda i: (0, 0))],
        out_specs=tok_spec,
        out_shape=jax.ShapeDtypeStruct((TOP_K, n), I32),
        compiler_params=_params(("arbitrary",)),
        name="moe_dest",
    )(idx, pos, start)


def _dispatch_kernel(dest_ref, h_ref, xs_ref, sem):
    tm = h_ref.shape[0]

    def issue(t, carry):
        for k in range(TOP_K):
            pltpu.make_async_copy(h_ref.at[pl.ds(t, 1), :], xs_ref.at[pl.ds(dest_ref[k, t], 1), :], sem).start()
        return carry

    lax.fori_loop(0, tm, issue, 0)
    for _ in range(TOP_K):
        pltpu.make_async_copy(h_ref, xs_ref.at[pl.ds(0, tm), :], sem).wait()


def _dispatch(h, dest):
    n, d = h.shape
    tm = TM_ROWS
    return pl.pallas_call(
        _dispatch_kernel,
        grid=(n // tm,),
        in_specs=[pl.BlockSpec((TOP_K, tm), lambda i: (0, i), memory_space=pltpu.SMEM),
                  pl.BlockSpec((tm, d), lambda i: (i, 0))],
        out_specs=pl.BlockSpec(memory_space=pl.ANY),
        out_shape=jax.ShapeDtypeStruct((n * TOP_K, d), U32),
        scratch_shapes=[pltpu.SemaphoreType.DMA(())],
        compiler_params=_params(("arbitrary",)),
        name="moe_dispatch",
    )(dest, h)


def _expert_kernel(tile_ref, exp_ref, lo_ref, hi_ref, newe_ref, newt_ref, last_ref,
                   xs_ref, w1_ref, w3_ref, w2_ref, ys_ref, w1_s, w3_s, w2_s, acc_s):
    i = pl.program_id(0)

    @pl.when(newe_ref[i] == 1)
    def _():
        w1_s[...] = w1_ref[0, 0].astype(BF16)
        w3_s[...] = w3_ref[0, 0].astype(BF16)
        w2_s[...] = w2_ref[0, 0].astype(BF16)

    tm = xs_ref.shape[0]
    lo, hi = lo_ref[i], hi_ref[i]

    @pl.when(hi > lo)
    def _():
        x_lo, x_hi = _unpack_rows(xs_ref[...])
        x = jnp.concatenate([x_lo, x_hi], axis=1).astype(BF16)
        a = jnp.dot(x, w1_s[...], preferred_element_type=F32)
        b = jnp.dot(x, w3_s[...], preferred_element_type=F32)
        hb = (a * jax.nn.sigmoid(a) * b).astype(BF16)
        y = jnp.dot(hb, w2_s[...], preferred_element_type=F32)
        row = tile_ref[i] * tm + lax.broadcasted_iota(I32, (tm, 1), 0)
        y = jnp.where((row >= lo) & (row < hi), y, 0.0)

        @pl.when(newt_ref[i] == 1)
        def _():
            acc_s[...] = y

        @pl.when(newt_ref[i] == 0)
        def _():
            acc_s[...] = acc_s[...] + y

    @pl.when(last_ref[i] == 1)
    def _():
        ys_ref[...] = _pack_rows(acc_s[...])


def _work_items(counts, n_rows, tm):
    ne = counts.shape[0]
    n_tiles = n_rows // tm
    n_items = n_tiles + ne - 1
    end = jnp.cumsum(counts)
    start = end - counts
    first_t = start // tm
    n_e = jnp.where(counts > 0, (end - 1) // tm - first_t + 1, 0)
    item_end = jnp.cumsum(n_e)
    item_off = item_end - n_e
    it = jnp.arange(n_items, dtype=I32)
    total = item_end[-1]
    valid = it < total
    last_item = jnp.minimum(it, total - 1)
    e = jnp.minimum(jnp.sum((item_end[None, :] <= last_item[:, None]).astype(I32), axis=1), ne - 1).astype(I32)
    tile = jnp.where(valid, first_t[e] + it - item_off[e], n_tiles - 1).astype(I32)
    lo = jnp.where(valid, jnp.maximum(start[e], tile * tm), 0).astype(I32)
    hi = jnp.where(valid, jnp.minimum(end[e], (tile + 1) * tm), 0).astype(I32)
    prev_e = jnp.concatenate([jnp.full((1,), -1, I32), e[:-1]])
    prev_t = jnp.concatenate([jnp.full((1,), -1, I32), tile[:-1]])
    next_t = jnp.concatenate([tile[1:], jnp.full((1,), -1, I32)])
    return (tile, e, lo, hi, (e != prev_e).astype(I32), (tile != prev_t).astype(I32),
            (tile != next_t).astype(I32))


def _experts(xs, items, w1, w3, w2, layer):
    n_rows, dw = xs.shape
    d, de = w1.shape[2], w1.shape[3]
    tm = TM_EXPERT
    n_items = items[0].shape[0]
    grid_spec = pltpu.PrefetchScalarGridSpec(
        num_scalar_prefetch=len(items),
        grid=(n_items,),
        in_specs=[pl.BlockSpec((tm, dw), lambda i, t, e, *_: (t[i], 0)),
                  pl.BlockSpec((1, 1, d, de), lambda i, t, e, *_: (layer, e[i], 0, 0)),
                  pl.BlockSpec((1, 1, d, de), lambda i, t, e, *_: (layer, e[i], 0, 0)),
                  pl.BlockSpec((1, 1, de, d), lambda i, t, e, *_: (layer, e[i], 0, 0))],
        out_specs=pl.BlockSpec((tm, dw), lambda i, t, e, *_: (t[i], 0)),
        scratch_shapes=[pltpu.VMEM((d, de), BF16), pltpu.VMEM((d, de), BF16), pltpu.VMEM((de, d), BF16),
                        pltpu.VMEM((tm, d), F32)],
    )
    return pl.pallas_call(
        _expert_kernel,
        grid_spec=grid_spec,
        out_shape=jax.ShapeDtypeStruct((n_rows, dw), U32),
        compiler_params=_params(("arbitrary",)),
        name="moe_experts",
    )(*items, xs, w1, w3, w2)


def _combine_kernel(dest_ref, x_ref, h_ref, w_ref, mod_ref, s1_ref, s3_ref, s2_ref, fg_ref, ys_ref, o_ref,
                    buf, sem, *, final):
    tm = x_ref.shape[0]

    def issue(t, carry):
        for k in range(TOP_K):
            pltpu.make_async_copy(ys_ref.at[pl.ds(dest_ref[k, t], 1), :], buf.at[k, pl.ds(t, 1), :], sem).start()
        return carry

    lax.fori_loop(0, tm, issue, 0)
    h_lo, h_hi = _unpack_rows(h_ref[...])
    hb = jnp.concatenate([h_lo, h_hi], axis=1).astype(BF16)
    a = jnp.dot(hb, s1_ref[...], preferred_element_type=F32)
    b = jnp.dot(hb, s3_ref[...], preferred_element_type=F32)
    acc = jnp.dot((a * jax.nn.sigmoid(a) * b).astype(BF16), s2_ref[...], preferred_element_type=F32)
    for k in range(TOP_K):
        pltpu.make_async_copy(ys_ref.at[pl.ds(0, tm), :], buf.at[k], sem).wait()
    w = w_ref[...]
    half = acc.shape[1] // 2
    acc_lo, acc_hi = acc[:, :half], acc[:, half:]
    for k in range(TOP_K):
        y_lo, y_hi = _unpack_rows(buf[k])
        acc_lo = acc_lo + w[:, k:k + 1] * y_lo
        acc_hi = acc_hi + w[:, k:k + 1] * y_hi
    acc = jnp.concatenate([acc_lo, acc_hi], axis=1)
    m = mod_ref[0]
    y = x_ref[...] + m[5:6] * acc
    if final:
        y = y * lax.rsqrt(jnp.mean(y * y, axis=-1, keepdims=True) + EPS) * fg_ref[...]
    o_ref[...] = y


def _combine(x, h, ys, dest, w, mod, s1, s3, s2, final_g, cmap, final):
    n, d = x.shape
    ds_ = s1.shape[1]
    tm = TM_ROWS
    return pl.pallas_call(
        functools.partial(_combine_kernel, final=final),
        grid=(n // tm,),
        in_specs=[pl.BlockSpec((TOP_K, tm), lambda i: (0, i), memory_space=pltpu.SMEM),
                  pl.BlockSpec((tm, d), lambda i: (i, 0)),
                  pl.BlockSpec((tm, d // 2), lambda i: (i, 0)),
                  pl.BlockSpec((tm, TOP_K), lambda i: (i, 0)),
                  pl.BlockSpec((1, N_MOD, d), lambda i: (cmap(i), 0, 0)),
                  pl.BlockSpec((d, ds_), lambda i: (0, 0)),
                  pl.BlockSpec((d, ds_), lambda i: (0, 0)),
                  pl.BlockSpec((ds_, d), lambda i: (0, 0)),
                  pl.BlockSpec((1, d), lambda i: (0, 0)),
                  pl.BlockSpec(memory_space=pl.ANY)],
        out_specs=pl.BlockSpec((tm, d), lambda i: (i, 0)),
        out_shape=jax.ShapeDtypeStruct((n, d), F32),
        scratch_shapes=[pltpu.VMEM((TOP_K, tm, d // 2), U32), pltpu.SemaphoreType.DMA(())],
        compiler_params=_params(("arbitrary",)),
        name="moe_combine",
    )(dest, x, h, w, mod, s1, s3, s2, final_g.reshape(1, d), ys)


def _moe(x, mod, layer, cmaps, p, final):
    cmap_r, cmap_c = cmaps
    n, d = x.shape
    rw = p['moe_router_w'][layer].T
    rw_hi = rw.astype(BF16)
    rw_lo = (rw - rw_hi.astype(F32)).astype(BF16)
    h, idx, wts, pos, cnt = _router(x, mod, p['norm2_g'][layer], rw_hi, rw_lo, p['moe_router_b'][layer], cmap_r)
    counts = cnt[:, 0].astype(I32)
    start = jnp.cumsum(counts) - counts
    dest = _dest(idx, pos, start.astype(F32).reshape(-1, 1))
    xs = _dispatch(h, dest)
    items = _work_items(counts, n * TOP_K, TM_EXPERT)
    ys = _experts(xs, items, p['moe_w1'], p['moe_w3'], p['moe_w2'], layer)
    return _combine(x, h, ys, dest, wts.T, mod,
                    p['moe_shared_w1'][layer].astype(BF16), p['moe_shared_w3'][layer].astype(BF16),
                    p['moe_shared_w2'][layer].astype(BF16), p['final_g'], cmap_c, final)


def kernel(x_prompt, x_sample, c, state_ssm, cache_k, cache_v, c_ctx, norm1_g, norm2_g, ada_w, ada_b, ssm_lam_re, ssm_lam_im, ssm_log_dt, ssm_b_re, ssm_b_im, ssm_c_re, ssm_c_im, ssm_d, ssm_glu_w, attn_w_qkv, attn_lam_q, attn_lam_k, attn_subln_g, attn_w_o, moe_router_w, moe_router_b, moe_w1, moe_w3, moe_w2, moe_shared_w1, moe_shared_w3, moe_shared_w2, final_g):
    p = dict(norm2_g=norm2_g, moe_router_w=moe_router_w, moe_router_b=moe_router_b,
             moe_w1=moe_w1, moe_w3=moe_w3, moe_w2=moe_w2, moe_shared_w1=moe_shared_w1,
             moe_shared_w3=moe_shared_w3, moe_shared_w2=moe_shared_w2, final_g=final_g)
    b_ctx, l_ctx, d = x_prompt.shape
    b_lat, l_lat, _ = x_sample.shape
    n_ctx, n_lat = b_ctx * l_ctx, b_lat * l_lat
    g = d // SSM_GROUP
    pdim = ssm_lam_re.shape[-1]
    nh = d // V_DIM
    assert 2 * pdim == LANES and CHUNK * SSM_GROUP == 2 * LANES
    assert n_ctx % TM_DENSE == 0 and l_lat % TM_DENSE == 0 and b_lat % 8 == 0 and b_ctx % 8 == 0
    nc_lat, nc_ctx = l_lat // CHUNK, l_ctx // CHUNK
    assert (b_lat * nc_lat) % (b_ctx * nc_ctx) == 0
    cmap = _cond_map(TM_DENSE, n_lat, l_lat)
    cmap_r = _cond_map(TM_ROUTER, n_lat, l_lat)
    cmap_c = _cond_map(TM_ROWS, n_lat, l_lat)

    nc = 1 + b_lat
    nc_pad = -(-nc // 8) * 8
    cond = jnp.concatenate([c_ctx[None, :], c, jnp.zeros((nc_pad - nc, d), F32)], axis=0)
    mod = _ada(cond, ada_w, ada_b)

    x = jnp.concatenate([x_sample.reshape(n_lat, d), x_prompt.reshape(n_ctx, d)], axis=0)

    u = _s5_pack(x, mod[0], norm1_g[0], cmap)
    wts = _s5_weights(ssm_lam_re[0], ssm_lam_im[0], ssm_log_dt[0], ssm_b_re[0], ssm_b_im[0],
                      ssm_c_re[0], ssm_c_im[0])
    h0 = state_ssm[:, 0].transpose(3, 0, 2, 1, 4).reshape(g, b_lat, 2 * LANES)
    y_lat, _ = _s5(u, h0, wts, 0, b_lat, nc_lat)
    y_ctx, fin = _s5(u, jnp.zeros((g, b_ctx, 2 * LANES), F32), wts, (b_lat * nc_lat) // (b_ctx * nc_ctx),
                     b_ctx, nc_ctx)
    new_state = fin.reshape(g, b_ctx, 2, 2, pdim).transpose(1, 3, 2, 0, 4)[:, None]
    x = _glu(x, y_lat, y_ctx, mod[0], norm1_g[0], ssm_d[0], ssm_glu_w[0].astype(BF16), cmap)
    x = _moe(x, mod[0], 0, (cmap_r, cmap_c), p, final=False)

    lam_init = 0.8 - 0.6 * math.exp(-0.3 * 1)
    wqkv = attn_w_qkv[0].astype(BF16)
    q_l, k_l, v_l = _qkv(x, mod[1], norm1_g[1], wqkv, cmap, 0, n_lat, rope_l=l_lat)
    q_c, k_c, v_c, kf, vf = _qkv(x, mod[1], norm1_g[1], wqkv, cmap, n_lat, n_ctx)
    new_k = kf.reshape(b_ctx, 1, l_ctx, nh, 2, HEAD_DIM)
    new_v = vf.reshape(b_ctx, 1, l_ctx, nh, V_DIM)
    o_c = _attention(q_c.reshape(b_ctx, l_ctx, d), k_c.reshape(b_ctx, l_ctx, d), v_c.reshape(b_ctx, l_ctx, d),
                     attn_lam_q[0], attn_lam_k[0], attn_subln_g[0], lam_init)
    past = cache_k.shape[2]
    k_all = jnp.concatenate([k_l.reshape(b_lat, l_lat, d), cache_k[:, 0].reshape(b_lat, past, d).astype(BF16)], axis=1)
    v_all = jnp.concatenate([v_l.reshape(b_lat, l_lat, d), cache_v[:, 0].reshape(b_lat, past, d).astype(BF16)], axis=1)
    o_l = _attention(q_l.reshape(b_lat, l_lat, d), k_all, v_all,
                     attn_lam_q[0], attn_lam_k[0], attn_subln_g[0], lam_init)
    o = jnp.concatenate([o_l.reshape(n_lat, d), o_c.reshape(n_ctx, d)], axis=0)
    x = _oproj(x, o, mod[1], attn_w_o[0].astype(BF16), cmap)
    y = _moe(x, mod[1], 1, (cmap_r, cmap_c), p, final=True)

    return (y[n_lat:].reshape(b_ctx, l_ctx, d), y[:n_lat].reshape(b_lat, l_lat, d),
            new_state, new_k, new_v)
```

```python
import functools
import math

import jax
import jax.numpy as jnp
from jax import lax
from jax.experimental import pallas as pl
from jax.experimental.pallas import tpu as pltpu

F32 = jnp.float32
BF16 = jnp.bfloat16
I32 = jnp.int32
U32 = jnp.uint32

EPS = 1e-6
SSM_GROUP = 16
CHUNK = 16
GRID_W = 64
HEAD_DIM = 64
V_DIM = 2 * HEAD_DIM
ROPE_THETA = 10000.0
TOP_K = 8
N_EXPERT_GROUPS = 8
TOPK_GROUPS = 4
ROUTED_SCALE = 2.5
N_MOD = 6

LANES = 128
VMEM_LIMIT = 48 * 1024 * 1024

TM_DENSE = 512
TM_ROUTER = 256
TM_EXPERT = 512
TM_ROWS = 128
TQ_ATTN = 256


def _params(sem):
    return pltpu.CompilerParams(dimension_semantics=sem, vmem_limit_bytes=VMEM_LIMIT)


def _normmod(x, g, sc, sh):
    ms = jnp.mean(x * x, axis=-1, keepdims=True)
    return (x * lax.rsqrt(ms + EPS) * g) * (1.0 + sc) + sh


def _cond_map(tm, n_lat, dec_seq):
    t_lat = n_lat // tm
    per = dec_seq // tm

    def f(i):
        return jnp.where(i < t_lat, 1 + i // per, 0)
    return f


def _ada_kernel(c_ref, w_ref, b_ref, o_ref):
    c = c_ref[...]
    s = (c * jax.nn.sigmoid(c)).astype(BF16)
    o_ref[0] = jnp.dot(s, w_ref[0].astype(BF16), preferred_element_type=F32) + b_ref[0]


def _ada(cond, ada_w, ada_b):
    depth, d, n6 = ada_w.shape
    nc = cond.shape[0]
    tn = 1536
    out = pl.pallas_call(
        _ada_kernel,
        grid=(depth, n6 // tn),
        in_specs=[pl.BlockSpec((nc, d), lambda l, j: (0, 0)),
                  pl.BlockSpec((1, d, tn), lambda l, j: (l, 0, j)),
                  pl.BlockSpec((1, 1, tn), lambda l, j: (l, 0, j))],
        out_specs=pl.BlockSpec((1, nc, tn), lambda l, j: (l, 0, j)),
        out_shape=jax.ShapeDtypeStruct((depth, nc, n6), F32),
        compiler_params=_params(("arbitrary", "arbitrary")),
        name="ada",
    )(cond, ada_w, ada_b.reshape(depth, 1, n6))
    return out.reshape(depth, nc, N_MOD, d)


GROUPS_PER_SLAB = LANES // SSM_GROUP
HALF_CHUNK = CHUNK // 2


def _chunk_perm():
    r = jnp.arange(HALF_CHUNK * LANES)
    t8, g8, c = r // LANES, (r // SSM_GROUP) % GROUPS_PER_SLAB, r % SSM_GROUP
    col = g8 * LANES + t8 * SSM_GROUP + c
    return (col[:, None] == r[None, :]).astype(BF16)


def _pack_kernel(x_ref, mod_ref, g_ref, p_ref, o_ref, h_s):
    m = mod_ref[0]
    h = _normmod(x_ref[...], g_ref[...], m[1:2], m[0:1])
    tm, d = h.shape
    nct = tm // CHUNK
    for j in range(d // LANES):
        h_s[j] = h[:, j * LANES:(j + 1) * LANES]
    slabs = []
    for j in range(d // LANES):
        for hf in range(2):
            slabs.append(jnp.concatenate(
                [h_s[j, pl.ds(hf * HALF_CHUNK + t8, nct, stride=CHUNK), :] for t8 in range(HALF_CHUNK)], axis=1))
    lhs = jnp.concatenate(slabs, axis=0).astype(BF16)
    out = jnp.dot(lhs, p_ref[...], preferred_element_type=F32)
    kw = CHUNK * SSM_GROUP
    for j in range(d // LANES):
        for hf in range(2):
            blk = out[(2 * j + hf) * nct:(2 * j + hf + 1) * nct]
            for g8 in range(GROUPS_PER_SLAB):
                col = (j * GROUPS_PER_SLAB + g8) * kw + hf * LANES
                o_ref[:, col:col + LANES] = blk[:, g8 * LANES:(g8 + 1) * LANES]


def _s5_pack(x, mod, g, cmap):
    n, d = x.shape
    tm = TM_DENSE
    return pl.pallas_call(
        _pack_kernel,
        grid=(n // tm,),
        in_specs=[pl.BlockSpec((tm, d), lambda i: (i, 0)),
                  pl.BlockSpec((1, N_MOD, d), lambda i: (cmap(i), 0, 0)),
                  pl.BlockSpec((1, d), lambda i: (0, 0)),
                  pl.BlockSpec((HALF_CHUNK * LANES, HALF_CHUNK * LANES), lambda i: (0, 0))],
        out_specs=pl.BlockSpec((tm // CHUNK, CHUNK * d), lambda i: (i, 0)),
        out_shape=jax.ShapeDtypeStruct((n // CHUNK, CHUNK * d), F32),
        scratch_shapes=[pltpu.VMEM((d // LANES, tm, LANES), F32)],
        compiler_params=_params(("arbitrary",)),
        name="s5_pack",
    )(x, mod, g.reshape(1, d), _chunk_perm())


def _s5_weights(lam_re, lam_im, log_dt, b_re, b_im, c_re, c_im):
    hi = lax.Precision.HIGHEST
    t = CHUNK
    w_in, w_out, toep, a16 = [], [], [], []
    for dr in range(2):
        dt = jnp.exp(log_dt[dr])[:, None]
        lr, li = lam_re[dr], lam_im[dr]
        mag = jnp.exp(lr * dt)
        a_re = mag * jnp.cos(li * dt)
        a_im = mag * jnp.sin(li * dt)
        den = lr * lr + li * li
        f_re = ((a_re - 1.0) * lr + a_im * li) / den
        f_im = (a_im * lr - (a_re - 1.0) * li) / den
        bb_re = f_re[..., None] * b_re[dr] - f_im[..., None] * b_im[dr]
        bb_im = f_re[..., None] * b_im[dr] + f_im[..., None] * b_re[dr]
        k = jnp.arange(t + 1, dtype=F32)[:, None, None]
        pm = jnp.exp(k * (lr * dt)[None])
        pk_re = pm * jnp.cos(k * (li * dt)[None])
        pk_im = pm * jnp.sin(k * (li * dt)[None])
        ks = jnp.arange(t)
        sel = (t - 1 - ks) if dr == 0 else ks
        pr, pi = pk_re[sel], pk_im[sel]
        wi_re = pr[..., None] * bb_re[None] - pi[..., None] * bb_im[None]
        wi_im = pr[..., None] * bb_im[None] + pi[..., None] * bb_re[None]
        g_, p_, c_ = bb_re.shape
        w_in.append((wi_re.transpose(1, 0, 3, 2).reshape(g_, t * c_, p_),
                     wi_im.transpose(1, 0, 3, 2).reshape(g_, t * c_, p_)))
        sel = (ks + 1) if dr == 0 else (t - ks)
        pr, pi = pk_re[sel], pk_im[sel]
        ca_re = c_re[dr][None] * pr[:, :, None, :] - c_im[dr][None] * pi[:, :, None, :]
        ca_im = c_re[dr][None] * pi[:, :, None, :] + c_im[dr][None] * pr[:, :, None, :]
        w_out.append((ca_re.transpose(1, 3, 0, 2).reshape(g_, p_, t * c_),
                      (-ca_im).transpose(1, 3, 0, 2).reshape(g_, p_, t * c_)))
        pr, pi = pk_re[:t], pk_im[:t]
        cd_re = c_re[dr][None] * pr[:, :, None, :] - c_im[dr][None] * pi[:, :, None, :]
        cd_im = c_re[dr][None] * pi[:, :, None, :] + c_im[dr][None] * pr[:, :, None, :]
        kk = (jnp.einsum('kgop,gpi->kgoi', cd_re, bb_re, precision=hi)
              - jnp.einsum('kgop,gpi->kgoi', cd_im, bb_im, precision=hi))
        s_i = ks[:, None]
        t_i = ks[None, :]
        delta = (t_i - s_i) if dr == 0 else (s_i - t_i)
        kd = jnp.where((delta >= 0)[:, :, None, None, None], kk[jnp.clip(delta, 0, t - 1)], 0.0)
        toep.append(kd.transpose(2, 0, 4, 1, 3).reshape(g_, t * c_, t * c_))
        a16.append((pk_re[t], pk_im[t]))
    w_in_all = jnp.concatenate([w_in[0][0], w_in[1][0], w_in[0][1], w_in[1][1]], axis=2)
    w_out_all = jnp.concatenate([w_out[0][0], w_out[1][0], w_out[0][1], w_out[1][1]], axis=1)
    toep_all = toep[0] + toep[1]
    a16_all = jnp.stack([jnp.concatenate([a16[0][0], a16[1][0]], axis=1),
                         jnp.concatenate([a16[0][1], a16[1][1]], axis=1)], axis=1)
    return w_in_all.astype(BF16), toep_all.astype(BF16), w_out_all.astype(BF16), a16_all


def _s5_kernel(u_hbm, win_ref, wtp_ref, wout_ref, a16_ref, h0_ref, y_hbm, fin_ref,
               ubuf, ybuf, v_s, sf_s, sb_s, sem_in, sem_out, *, nc, nb, row0):
    g = pl.program_id(0)
    ng = pl.num_programs(0)
    kw = ubuf.shape[3]
    slot = g % 2

    def in_copy(grp, s, b):
        return pltpu.make_async_copy(u_hbm.at[pl.ds(row0 + b * nc, nc), pl.ds(grp * kw, kw)],
                                     ubuf.at[s, :, b, :], sem_in.at[s])

    def out_copy(grp, s, b):
        return pltpu.make_async_copy(ybuf.at[s, :, b, :],
                                     y_hbm.at[pl.ds(b * nc, nc), pl.ds(grp * kw, kw)], sem_out.at[s])

    @pl.when(g == 0)
    def _():
        for b in range(nb):
            in_copy(g, slot, b).start()

    @pl.when(g + 1 < ng)
    def _():
        for b in range(nb):
            in_copy(g + 1, 1 - slot, b).start()

    for b in range(nb):
        in_copy(g, slot, b).wait()
    u = ubuf[slot].reshape(nc * nb, kw).astype(BF16)
    v_s[...] = jnp.dot(u, win_ref[0], preferred_element_type=F32)
    a = a16_ref[0]
    ar, ai = a[0:1], a[1:2]
    fwd = (lax.broadcasted_iota(I32, (nb, 2 * LANES), 1) % LANES) < (LANES // 2)

    def step(i, carry):
        sr, si = carry
        st = jnp.concatenate([sr, si], axis=1)
        rf = pl.multiple_of(i * nb, nb)
        rb = pl.multiple_of((nc - 1 - i) * nb, nb)
        sf_s[pl.ds(rf, nb), :] = st
        sb_s[pl.ds(rb, nb), :] = st
        v = jnp.where(fwd, v_s[pl.ds(rf, nb), :], v_s[pl.ds(rb, nb), :])
        return ar * sr - ai * si + v[:, :LANES], ar * si + ai * sr + v[:, LANES:]

    h0 = h0_ref[0]
    sr, si = lax.fori_loop(0, nc, step, (h0[:, :LANES], h0[:, LANES:]))
    fin_ref[0] = jnp.concatenate([sr, si], axis=1)
    fwd_all = (lax.broadcasted_iota(I32, (nc * nb, 2 * LANES), 1) % LANES) < (LANES // 2)
    hs = jnp.where(fwd_all, sf_s[...], sb_s[...]).astype(BF16)
    y = (jnp.dot(u, wtp_ref[0], preferred_element_type=F32)
         + jnp.dot(hs, wout_ref[0], preferred_element_type=F32))

    @pl.when(g >= 2)
    def _():
        for b in range(nb):
            out_copy(g - 2, slot, b).wait()

    ybuf[slot] = y.reshape(nc, nb, kw)
    for b in range(nb):
        out_copy(g, slot, b).start()

    @pl.when(g == ng - 1)
    def _():
        if ng > 1:
            for b in range(nb):
                out_copy(g - 1, 1 - slot, b).wait()
        for b in range(nb):
            out_copy(g, slot, b).wait()


def _s5(u, h0, wts, row0, nb, nc):
    w_in, w_tp, w_out, a16 = wts
    r = nb * nc
    kw = CHUNK * SSM_GROUP
    g = u.shape[1] // kw
    return pl.pallas_call(
        functools.partial(_s5_kernel, nc=nc, nb=nb, row0=row0),
        grid=(g,),
        in_specs=[pl.BlockSpec(memory_space=pl.ANY),
                  pl.BlockSpec((1, kw, kw), lambda j: (j, 0, 0)),
                  pl.BlockSpec((1, kw, kw), lambda j: (j, 0, 0)),
                  pl.BlockSpec((1, kw, kw), lambda j: (j, 0, 0)),
                  pl.BlockSpec((1, 2, LANES), lambda j: (j, 0, 0)),
                  pl.BlockSpec((1, nb, kw), lambda j: (j, 0, 0))],
        out_specs=[pl.BlockSpec(memory_space=pl.ANY),
                   pl.BlockSpec((1, nb, kw), lambda j: (j, 0, 0))],
        out_shape=[jax.ShapeDtypeStruct((r, u.shape[1]), F32),
                   jax.ShapeDtypeStruct((g, nb, kw), F32)],
        scratch_shapes=[pltpu.VMEM((2, nc, nb, kw), F32), pltpu.VMEM((2, nc, nb, kw), F32),
                        pltpu.VMEM((r, kw), F32), pltpu.VMEM((r, kw), F32), pltpu.VMEM((r, kw), F32),
                        pltpu.SemaphoreType.DMA((2,)), pltpu.SemaphoreType.DMA((2,))],
        compiler_params=_params(("arbitrary",)),
        name="s5_scan",
    )(u, w_in, w_tp, w_out, a16, h0)


def _glu_kernel(x_ref, ya_ref, yb_ref, mod_ref, g_ref, d_ref, pt_ref, w_ref, o_ref, y_s, *, tiles_a):
    tm, dd = x_ref.shape
    nct = tm // CHUNK
    kw = CHUNK * SSM_GROUP
    from_a = pl.program_id(0) < tiles_a
    slabs = []
    for j in range(dd // LANES):
        for hf in range(2):
            cols = [(j * GROUPS_PER_SLAB + g8) * kw + hf * LANES for g8 in range(GROUPS_PER_SLAB)]
            slabs.append(jnp.concatenate(
                [jnp.where(from_a, ya_ref[:, c0:c0 + LANES], yb_ref[:, c0:c0 + LANES]) for c0 in cols], axis=1))
    lhs = jnp.concatenate(slabs, axis=0)
    hi = lhs.astype(BF16)
    lo = (lhs - hi.astype(F32)).astype(BF16)
    pt = pt_ref[...]
    nat = jnp.dot(hi, pt, preferred_element_type=F32) + jnp.dot(lo, pt, preferred_element_type=F32)
    for j in range(dd // LANES):
        for hf in range(2):
            blk = nat[(2 * j + hf) * nct:(2 * j + hf + 1) * nct]
            for t8 in range(HALF_CHUNK):
                y_s[j, pl.ds(hf * HALF_CHUNK + t8, nct, stride=CHUNK), :] = blk[:, t8 * LANES:(t8 + 1) * LANES]
    m = mod_ref[0]
    x = x_ref[...]
    h = _normmod(x, g_ref[...], m[1:2], m[0:1])
    y = d_ref[...] * h + jnp.concatenate([y_s[j] for j in range(dd // LANES)], axis=1)
    gl = jax.nn.gelu(y).astype(BF16)
    z = jnp.dot(gl, w_ref[...], preferred_element_type=F32)
    o_ref[...] = x + m[2:3] * (z[:, :dd] * jax.nn.sigmoid(z[:, dd:]))


def _glu(x, y_a, y_b, mod, g, dskip, w, cmap):
    n, d = x.shape
    tm = TM_DENSE
    nct = tm // CHUNK
    tiles_a = y_a.shape[0] // nct
    last_a = tiles_a - 1
    return pl.pallas_call(
        functools.partial(_glu_kernel, tiles_a=tiles_a),
        grid=(n // tm,),
        in_specs=[pl.BlockSpec((tm, d), lambda i: (i, 0)),
                  pl.BlockSpec((nct, CHUNK * d), lambda i: (jnp.minimum(i, last_a), 0)),
                  pl.BlockSpec((nct, CHUNK * d), lambda i: (jnp.maximum(i - tiles_a, 0), 0)),
                  pl.BlockSpec((1, N_MOD, d), lambda i: (cmap(i), 0, 0)),
                  pl.BlockSpec((1, d), lambda i: (0, 0)),
                  pl.BlockSpec((1, d), lambda i: (0, 0)),
                  pl.BlockSpec((HALF_CHUNK * LANES, HALF_CHUNK * LANES), lambda i: (0, 0)),
                  pl.BlockSpec((d, 2 * d), lambda i: (0, 0))],
        out_specs=pl.BlockSpec((tm, d), lambda i: (i, 0)),
        out_shape=jax.ShapeDtypeStruct((n, d), F32),
        scratch_shapes=[pltpu.VMEM((d // LANES, tm, LANES), F32)],
        compiler_params=_params(("arbitrary",)),
        name="s5_glu",
    )(x, y_a, y_b, mod, g.reshape(1, d), dskip.reshape(1, d), _chunk_perm().T, w)


def _rope_tables(l):
    r = HEAD_DIM // 2
    half = r // 2
    freq = ROPE_THETA ** (-jnp.arange(half, dtype=F32) / half)
    pos = jnp.arange(l)
    row = (pos // GRID_W).astype(F32)
    col = (pos % GRID_W).astype(F32)
    d = jnp.arange(LANES) % HEAD_DIM
    p = jnp.where((d < r)[None, :], row[:, None], col[:, None])
    ang = p * freq[d % half][None, :]
    sign = jnp.where((d % r) < half, -1.0, 1.0)[None, :]
    return jnp.cos(ang), jnp.sin(ang) * sign


def _qkv_kernel(*refs, rope):
    if rope:
        x_ref, mod_ref, g_ref, w_ref, cos_ref, sin_ref, q_ref, k_ref, v_ref = refs
    else:
        x_ref, mod_ref, g_ref, w_ref, q_ref, k_ref, v_ref, kf_ref, vf_ref = refs
    m = mod_ref[0]
    h = _normmod(x_ref[...], g_ref[...], m[1:2], m[0:1]).astype(BF16)
    qkv = jnp.dot(h, w_ref[...], preferred_element_type=F32)
    d = x_ref.shape[1]
    q = qkv[:, :d] * (HEAD_DIM ** -0.5 * math.log2(math.e))
    k = qkv[:, d:2 * d]
    v = qkv[:, 2 * d:]
    v_ref[...] = v.astype(BF16)
    if not rope:
        q_ref[...] = q.astype(BF16)
        k_ref[...] = k.astype(BF16)
        kf_ref[...] = k
        vf_ref[...] = v
        return
    cos = cos_ref[...]
    sin = sin_ref[...]
    half = HEAD_DIM // 4
    lane = lax.broadcasted_iota(I32, cos.shape, 1)
    up_sel = (lane % (2 * half)) < half
    for src, dst in ((q, q_ref), (k, k_ref)):
        for j in range(d // LANES):
            blk = src[:, j * LANES:(j + 1) * LANES]
            up = pltpu.roll(blk, LANES - half, 1)
            dn = pltpu.roll(blk, half, 1)
            dst[:, j * LANES:(j + 1) * LANES] = (blk * cos + jnp.where(up_sel, up, dn) * sin).astype(BF16)


def _qkv(x, mod, g, w, cmap, row0, nrows, rope_l=None):
    n, d = x.shape
    tm = TM_DENSE
    t0 = row0 // tm
    rope = rope_l is not None
    in_specs = [pl.BlockSpec((tm, d), lambda i: (i + t0, 0)),
                pl.BlockSpec((1, N_MOD, d), lambda i: (cmap(i + t0), 0, 0)),
                pl.BlockSpec((1, d), lambda i: (0, 0)),
                pl.BlockSpec((d, 3 * d), lambda i: (0, 0))]
    args = [x, mod, g.reshape(1, d), w]
    row_spec = pl.BlockSpec((tm, d), lambda i: (i, 0))
    out_specs = [row_spec] * 3
    out_shape = [jax.ShapeDtypeStruct((nrows, d), BF16)] * 3
    if rope:
        per = rope_l // tm
        cos, sin = _rope_tables(rope_l)
        in_specs += [pl.BlockSpec((tm, LANES), lambda i: (i % per, 0))] * 2
        args += [cos, sin]
    else:
        out_specs += [row_spec] * 2
        out_shape += [jax.ShapeDtypeStruct((nrows, d), F32)] * 2
    return pl.pallas_call(
        functools.partial(_qkv_kernel, rope=rope),
        grid=(nrows // tm,),
        in_specs=in_specs,
        out_specs=out_specs,
        out_shape=out_shape,
        compiler_params=_params(("arbitrary",)),
        name="qkv_rope" if rope else "qkv",
    )(*args)


def _attn_kernel(q_ref, kt_ref, v_ref, lq_ref, lk_ref, g_ref, o_ref, *, lam_init):
    q = q_ref[0]
    tq = q.shape[0]
    lane = lax.broadcasted_iota(I32, q.shape, 1)
    zero = jnp.zeros_like(q)
    qq = jnp.concatenate([jnp.where(lane < HEAD_DIM, q, zero), jnp.where(lane >= HEAD_DIM, q, zero)], axis=0)
    s = jnp.dot(qq, kt_ref[0, 0], preferred_element_type=F32)
    p = jnp.exp2(s - jnp.max(s, axis=1, keepdims=True))
    l = jnp.sum(p, axis=1, keepdims=True)
    lqk = lq_ref[...] * lk_ref[...]
    lam = (jnp.exp(jnp.sum(lqk[0:1], axis=1, keepdims=True))
           - jnp.exp(jnp.sum(lqk[1:2], axis=1, keepdims=True)) + lam_init)
    w = p[:tq] * (1.0 / l[:tq]) - p[tq:] * (lam / l[tq:])
    o = jnp.dot(w.astype(BF16), v_ref[0], preferred_element_type=F32)
    o = o * lax.rsqrt(jnp.mean(o * o, axis=1, keepdims=True) + EPS) * g_ref[...]
    o_ref[0] = (o * (1.0 - lam_init)).astype(o_ref.dtype)


def _attention(q, k, v, lam_q, lam_k, subln_g, lam_init):
    nb, l, d = q.shape
    lk = k.shape[1]
    nh = d // V_DIM
    kt = k.reshape(nb, lk, nh, V_DIM).transpose(0, 2, 3, 1)
    tq = min(TQ_ATTN, l)
    return pl.pallas_call(
        functools.partial(_attn_kernel, lam_init=lam_init),
        grid=(nb, nh, l // tq),
        in_specs=[pl.BlockSpec((1, tq, V_DIM), lambda b, h, i: (b, i, h)),
                  pl.BlockSpec((1, 1, V_DIM, lk), lambda b, h, i: (b, h, 0, 0)),
                  pl.BlockSpec((1, lk, V_DIM), lambda b, h, i: (b, 0, h)),
                  pl.BlockSpec((2, HEAD_DIM), lambda b, h, i: (0, 0)),
                  pl.BlockSpec((2, HEAD_DIM), lambda b, h, i: (0, 0)),
                  pl.BlockSpec((1, V_DIM), lambda b, h, i: (0, 0))],
        out_specs=pl.BlockSpec((1, tq, V_DIM), lambda b, h, i: (b, i, h)),
        out_shape=jax.ShapeDtypeStruct((nb, l, d), BF16),
        compiler_params=_params(("arbitrary", "arbitrary", "arbitrary")),
        name="diff_attn",
    )(q, kt, v, lam_q, lam_k, subln_g.reshape(1, V_DIM))


def _oproj_kernel(x_ref, o_ref, mod_ref, w_ref, y_ref):
    m = mod_ref[0]
    y_ref[...] = x_ref[...] + m[2:3] * jnp.dot(o_ref[...], w_ref[...], preferred_element_type=F32)


def _oproj(x, o, mod, w, cmap):
    n, d = x.shape
    tm = TM_DENSE
    return pl.pallas_call(
        _oproj_kernel,
        grid=(n // tm,),
        in_specs=[pl.BlockSpec((tm, d), lambda i: (i, 0)),
                  pl.BlockSpec((tm, d), lambda i: (i, 0)),
                  pl.BlockSpec((1, N_MOD, d), lambda i: (cmap(i), 0, 0)),
                  pl.BlockSpec((d, d), lambda i: (0, 0))],
        out_specs=pl.BlockSpec((tm, d), lambda i: (i, 0)),
        out_shape=jax.ShapeDtypeStruct((n, d), F32),
        compiler_params=_params(("arbitrary",)),
        name="attn_out",
    )(x, o, mod, w)


def _pack_rows(h):
    half = h.shape[1] // 2
    bits = lax.bitcast_convert_type(h.astype(BF16).astype(F32), U32)
    return bits[:, half:] | (bits[:, :half] >> 16)


def _unpack_rows(w):
    lo = lax.bitcast_convert_type(w << 16, F32)
    hi = lax.bitcast_convert_type(w & jnp.uint32(0xFFFF0000), F32)
    return lo, hi


def _stack_rows(rows, width):
    ri = lax.broadcasted_iota(I32, (len(rows), width), 0)
    out = jnp.zeros((len(rows), width), rows[0].dtype)
    for j, r in enumerate(rows):
        out = jnp.where(ri == j, r, out)
    return out


def _router_kernel(x_ref, mod_ref, g_ref, wh_ref, wl_ref, b_ref, h_ref, idx_ref, w_ref, pos_ref, cnt_ref, cnt_s):
    step = pl.program_id(0)

    @pl.when(step == 0)
    def _():
        cnt_s[...] = jnp.zeros_like(cnt_s)

    m = mod_ref[0]
    h = _normmod(x_ref[...], g_ref[...], m[4:5], m[3:4])
    h_ref[...] = _pack_rows(h)
    tr = h.shape[0]
    ne = wh_ref.shape[0]
    per = ne // N_EXPERT_GROUPS
    hh = h.astype(BF16)
    hl = (h - hh.astype(F32)).astype(BF16)
    nt = (((1,), (1,)), ((), ()))
    wh = wh_ref[...]
    logits = (lax.dot_general(wh, hh, nt, preferred_element_type=F32)
              + lax.dot_general(wl_ref[...], hh, nt, preferred_element_type=F32)
              + lax.dot_general(wh, hl, nt, preferred_element_type=F32))
    scores = jax.nn.sigmoid(logits)
    biased = scores + b_ref[...]
    neg = -jnp.inf
    ri = lax.broadcasted_iota(I32, (per, tr), 0).astype(F32)
    blocks, gscores = [], []
    for j in range(N_EXPERT_GROUPS):
        blk = biased[j * per:(j + 1) * per, :]
        m1 = jnp.max(blk, axis=0, keepdims=True)
        f1 = jnp.min(jnp.where(blk == m1, ri, float(per)), axis=0, keepdims=True)
        m2 = jnp.max(jnp.where(ri == f1, neg, blk), axis=0, keepdims=True)
        blocks.append(blk)
        gscores.append(m1 + m2)
    gs = _stack_rows(gscores, tr)
    gi = lax.broadcasted_iota(I32, gs.shape, 0).astype(F32)
    gsel = jnp.zeros(gs.shape, F32)
    for _ in range(TOPK_GROUPS):
        mm = jnp.max(gs, axis=0, keepdims=True)
        pick = gi == jnp.min(jnp.where(gs == mm, gi, float(N_EXPERT_GROUPS)), axis=0, keepdims=True)
        gsel = jnp.where(pick, 1.0, gsel)
        gs = jnp.where(pick, neg, gs)
    masked = jnp.concatenate(
        [jnp.where(gsel[j:j + 1, :] > 0.0, blocks[j], neg) for j in range(N_EXPERT_GROUPS)], axis=0)
    ei = lax.broadcasted_iota(I32, masked.shape, 0).astype(F32)
    picks, idxs, ws = [], [], []
    for _ in range(TOP_K):
        mm = jnp.max(masked, axis=0, keepdims=True)
        ff = jnp.min(jnp.where(masked == mm, ei, float(ne)), axis=0, keepdims=True)
        pick = ei == ff
        picks.append(pick)
        idxs.append(ff)
        ws.append(jnp.sum(jnp.where(pick, scores, 0.0), axis=0, keepdims=True))
        masked = jnp.where(pick, neg, masked)
    wsum = ws[0]
    for wk in ws[1:]:
        wsum = wsum + wk
    w_ref[...] = _stack_rows([wk / wsum * ROUTED_SCALE for wk in ws], tr)
    idx_ref[...] = _stack_rows(idxs, tr).astype(I32)
    sel = jnp.zeros(masked.shape, F32)
    for pick in picks:
        sel = sel + jnp.where(pick, 1.0, 0.0)
    earlier = (lax.broadcasted_iota(I32, (tr, tr), 0) < lax.broadcasted_iota(I32, (tr, tr), 1))
    prefix = jnp.dot(sel.astype(BF16), jnp.where(earlier, 1.0, 0.0).astype(BF16), preferred_element_type=F32)
    base = prefix + cnt_s[...]
    pos_ref[...] = _stack_rows(
        [jnp.sum(jnp.where(pick, base, 0.0), axis=0, keepdims=True) for pick in picks], tr).astype(I32)
    cnt_s[...] = cnt_s[...] + jnp.sum(sel, axis=1, keepdims=True)
    cnt_ref[...] = cnt_s[...]


def _router(x, mod, g, w_t_hi, w_t_lo, bias, cmap):
    n, d = x.shape
    ne = w_t_hi.shape[0]
    tr = TM_ROUTER
    tok_spec = pl.BlockSpec((TOP_K, tr), lambda i: (0, i))
    return pl.pallas_call(
        _router_kernel,
        grid=(n // tr,),
        in_specs=[pl.BlockSpec((tr, d), lambda i: (i, 0)),
                  pl.BlockSpec((1, N_MOD, d), lambda i: (cmap(i), 0, 0)),
                  pl.BlockSpec((1, d), lambda i: (0, 0)),
                  pl.BlockSpec((ne, d), lambda i: (0, 0)),
                  pl.BlockSpec((ne, d), lambda i: (0, 0)),
                  pl.BlockSpec((ne, 1), lambda i: (0, 0))],
        out_specs=[pl.BlockSpec((tr, d // 2), lambda i: (i, 0)), tok_spec, tok_spec, tok_spec,
                   pl.BlockSpec((ne, 1), lambda i: (0, 0))],
        out_shape=[jax.ShapeDtypeStruct((n, d // 2), U32),
                   jax.ShapeDtypeStruct((TOP_K, n), I32),
                   jax.ShapeDtypeStruct((TOP_K, n), F32),
                   jax.ShapeDtypeStruct((TOP_K, n), I32),
                   jax.ShapeDtypeStruct((ne, 1), F32)],
        scratch_shapes=[pltpu.VMEM((ne, 1), F32)],
        compiler_params=_params(("arbitrary",)),
        name="moe_router",
    )(x, mod, g.reshape(1, d), w_t_hi, w_t_lo, bias.reshape(ne, 1))


def _dest_kernel(idx_ref, pos_ref, start_ref, o_ref):
    idx = idx_ref[...]
    ei = lax.broadcasted_iota(I32, (start_ref.shape[0], idx.shape[1]), 0)
    start = start_ref[...]
    rows = [jnp.sum(jnp.where(ei == idx[k:k + 1], start, 0.0), axis=0, keepdims=True) for k in range(TOP_K)]
    o_ref[...] = _stack_rows(rows, idx.shape[1]).astype(I32) + pos_ref[...]


def _dest(idx, pos, start):
    n = idx.shape[1]
    ne = start.shape[0]
    tr = 512
    tok_spec = pl.BlockSpec((TOP_K, tr), lambda i: (0, i))
    return pl.pallas_call(
        _dest_kernel,
        grid=(n // tr,),
        in_specs=[tok_spec, tok_spec, pl.BlockSpec((ne, 1), lambda i: (0, 0))],
        out_specs=tok_spec,
        out_shape=jax.ShapeDtypeStruct((TOP_K, n), I32),
        compiler_params=_params(("arbitrary",)),
        name="moe_dest",
    )(idx, pos, start)


def _dispatch_kernel(dest_ref, h_ref, xs_ref, sem):
    tm = h_ref.shape[0]

    def issue(t, carry):
        for k in range(TOP_K):
            pltpu.make_async_copy(h_ref.at[pl.ds(t, 1), :], xs_ref.at[pl.ds(dest_ref[k, t], 1), :], sem).start()
        return carry

    lax.fori_loop(0, tm, issue, 0)
    for _ in range(TOP_K):
        pltpu.make_async_copy(h_ref, xs_ref.at[pl.ds(0, tm), :], sem).wait()


def _dispatch(h, dest):
    n, d = h.shape
    tm = TM_ROWS
    return pl.pallas_call(
        _dispatch_kernel,
        grid=(n // tm,),
        in_specs=[pl.BlockSpec((TOP_K, tm), lambda i: (0, i), memory_space=pltpu.SMEM),
                  pl.BlockSpec((tm, d), lambda i: (i, 0))],
        out_specs=pl.BlockSpec(memory_space=pl.ANY),
        out_shape=jax.ShapeDtypeStruct((n * TOP_K, d), U32),
        scratch_shapes=[pltpu.SemaphoreType.DMA(())],
        compiler_params=_params(("arbitrary",)),
        name="moe_dispatch",
    )(dest, h)


def _expert_kernel(tile_ref, exp_ref, lo_ref, hi_ref, newe_ref, newt_ref, last_ref,
                   xs_ref, w1_ref, w3_ref, w2_ref, ys_ref, w1_s, w3_s, w2_s, acc_s):
    i = pl.program_id(0)

    @pl.when(newe_ref[i] == 1)
    def _():
        w1_s[...] = w1_ref[0, 0].astype(BF16)
        w3_s[...] = w3_ref[0, 0].astype(BF16)
        w2_s[...] = w2_ref[0, 0].astype(BF16)

    tm = xs_ref.shape[0]
    lo, hi = lo_ref[i], hi_ref[i]

    @pl.when(hi > lo)
    def _():
        x_lo, x_hi = _unpack_rows(xs_ref[...])
        x = jnp.concatenate([x_lo, x_hi], axis=1).astype(BF16)
        a = jnp.dot(x, w1_s[...], preferred_element_type=F32)
        b = jnp.dot(x, w3_s[...], preferred_element_type=F32)
        hb = (a * jax.nn.sigmoid(a) * b).astype(BF16)
        y = jnp.dot(hb, w2_s[...], preferred_element_type=F32)
        row = tile_ref[i] * tm + lax.broadcasted_iota(I32, (tm, 1), 0)
        y = jnp.where((row >= lo) & (row < hi), y, 0.0)

        @pl.when(newt_ref[i] == 1)
        def _():
            acc_s[...] = y

        @pl.when(newt_ref[i] == 0)
        def _():
            acc_s[...] = acc_s[...] + y

    @pl.when(last_ref[i] == 1)
    def _():
        ys_ref[...] = _pack_rows(acc_s[...])


def _work_items(counts, n_rows, tm):
    ne = counts.shape[0]
    n_tiles = n_rows // tm
    n_items = n_tiles + ne - 1
    end = jnp.cumsum(counts)
    start = end - counts
    first_t = start // tm
    n_e = jnp.where(counts > 0, (end - 1) // tm - first_t + 1, 0)
    item_end = jnp.cumsum(n_e)
    item_off = item_end - n_e
    it = jnp.arange(n_items, dtype=I32)
    total = item_end[-1]
    valid = it < total
    last_item = jnp.minimum(it, total - 1)
    e = jnp.minimum(jnp.sum((item_end[None, :] <= last_item[:, None]).astype(I32), axis=1), ne - 1).astype(I32)
    tile = jnp.where(valid, first_t[e] + it - item_off[e], n_tiles - 1).astype(I32)
    lo = jnp.where(valid, jnp.maximum(start[e], tile * tm), 0).astype(I32)
    hi = jnp.where(valid, jnp.minimum(end[e], (tile + 1) * tm), 0).astype(I32)
    prev_e = jnp.concatenate([jnp.full((1,), -1, I32), e[:-1]])
    prev_t = jnp.concatenate([jnp.full((1,), -1, I32), tile[:-1]])
    next_t = jnp.concatenate([tile[1:], jnp.full((1,), -1, I32)])
    return (tile, e, lo, hi, (e != prev_e).astype(I32), (tile != prev_t).astype(I32),
            (tile != next_t).astype(I32))


def _experts(xs, items, w1, w3, w2, layer):
    n_rows, dw = xs.shape
    d, de = w1.shape[2], w1.shape[3]
    tm = TM_EXPERT
    n_items = items[0].shape[0]
    grid_spec = pltpu.PrefetchScalarGridSpec(
        num_scalar_prefetch=len(items),
        grid=(n_items,),
        in_specs=[pl.BlockSpec((tm, dw), lambda i, t, e, *_: (t[i], 0)),
                  pl.BlockSpec((1, 1, d, de), lambda i, t, e, *_: (layer, e[i], 0, 0)),
                  pl.BlockSpec((1, 1, d, de), lambda i, t, e, *_: (layer, e[i], 0, 0)),
                  pl.BlockSpec((1, 1, de, d), lambda i, t, e, *_: (layer, e[i], 0, 0))],
        out_specs=pl.BlockSpec((tm, dw), lambda i, t, e, *_: (t[i], 0)),
        scratch_shapes=[pltpu.VMEM((d, de), BF16), pltpu.VMEM((d, de), BF16), pltpu.VMEM((de, d), BF16),
                        pltpu.VMEM((tm, d), F32)],
    )
    return pl.pallas_call(
        _expert_kernel,
        grid_spec=grid_spec,
        out_shape=jax.ShapeDtypeStruct((n_rows, dw), U32),
        compiler_params=_params(("arbitrary",)),
        name="moe_experts",
    )(*items, xs, w1, w3, w2)


def _combine_kernel(dest_ref, dest_next_ref, x_ref, h_ref, w_ref, mod_ref, s1_ref, s3_ref, s2_ref, fg_ref, ys_ref,
                    o_ref, buf, sem, *, final):
    i = pl.program_id(0)
    tm = x_ref.shape[0]
    slot = i % 2

    def issue(dref, s):
        def body(t, carry):
            for k in range(TOP_K):
                pltpu.make_async_copy(ys_ref.at[pl.ds(dref[k, t], 1), :], buf.at[s, k, pl.ds(t, 1), :],
                                      sem.at[s]).start()
            return carry
        lax.fori_loop(0, tm, body, 0)

    @pl.when(i == 0)
    def _():
        issue(dest_ref, slot)

    @pl.when(i + 1 < pl.num_programs(0))
    def _():
        issue(dest_next_ref, 1 - slot)

    h_lo, h_hi = _unpack_rows(h_ref[...])
    hb = jnp.concatenate([h_lo, h_hi], axis=1).astype(BF16)
    a = jnp.dot(hb, s1_ref[...], preferred_element_type=F32)
    b = jnp.dot(hb, s3_ref[...], preferred_element_type=F32)
    acc = jnp.dot((a * jax.nn.sigmoid(a) * b).astype(BF16), s2_ref[...], preferred_element_type=F32)
    for k in range(TOP_K):
        pltpu.make_async_copy(ys_ref.at[pl.ds(0, tm), :], buf.at[slot, k], sem.at[slot]).wait()
    w = w_ref[...]
    half = acc.shape[1] // 2
    acc_lo, acc_hi = acc[:, :half], acc[:, half:]
    for k in range(TOP_K):
        y_lo, y_hi = _unpack_rows(buf[slot, k])
        acc_lo = acc_lo + w[:, k:k + 1] * y_lo
        acc_hi = acc_hi + w[:, k:k + 1] * y_hi
    acc = jnp.concatenate([acc_lo, acc_hi], axis=1)
    m = mod_ref[0]
    y = x_ref[...] + m[5:6] * acc
    if final:
        y = y * lax.rsqrt(jnp.mean(y * y, axis=-1, keepdims=True) + EPS) * fg_ref[...]
    o_ref[...] = y


def _combine(x, h, ys, dest, w, mod, s1, s3, s2, final_g, cmap, final):
    n, d = x.shape
    ds_ = s1.shape[1]
    tm = TM_ROWS
    last = n // tm - 1
    return pl.pallas_call(
        functools.partial(_combine_kernel, final=final),
        grid=(n // tm,),
        in_specs=[pl.BlockSpec((TOP_K, tm), lambda i: (0, i), memory_space=pltpu.SMEM),
                  pl.BlockSpec((TOP_K, tm), lambda i: (0, jnp.minimum(i + 1, last)), memory_space=pltpu.SMEM),
                  pl.BlockSpec((tm, d), lambda i: (i, 0)),
                  pl.BlockSpec((tm, d // 2), lambda i: (i, 0)),
                  pl.BlockSpec((tm, TOP_K), lambda i: (i, 0)),
                  pl.BlockSpec((1, N_MOD, d), lambda i: (cmap(i), 0, 0)),
                  pl.BlockSpec((d, ds_), lambda i: (0, 0)),
                  pl.BlockSpec((d, ds_), lambda i: (0, 0)),
                  pl.BlockSpec((ds_, d), lambda i: (0, 0)),
                  pl.BlockSpec((1, d), lambda i: (0, 0)),
                  pl.BlockSpec(memory_space=pl.ANY)],
        out_specs=pl.BlockSpec((tm, d), lambda i: (i, 0)),
        out_shape=jax.ShapeDtypeStruct((n, d), F32),
        scratch_shapes=[pltpu.VMEM((2, TOP_K, tm, d // 2), U32), pltpu.SemaphoreType.DMA((2,))],
        compiler_params=_params(("arbitrary",)),
        name="moe_combine",
    )(dest, dest, x, h, w, mod, s1, s3, s2, final_g.reshape(1, d), ys)


def _moe(x, mod, layer, cmaps, p, final):
    cmap_r, cmap_c = cmaps
    n, d = x.shape
    rw = p['moe_router_w'][layer].T
    rw_hi = rw.astype(BF16)
    rw_lo = (rw - rw_hi.astype(F32)).astype(BF16)
    h, idx, wts, pos, cnt = _router(x, mod, p['norm2_g'][layer], rw_hi, rw_lo, p['moe_router_b'][layer], cmap_r)
    counts = cnt[:, 0].astype(I32)
    start = jnp.cumsum(counts) - counts
    dest = _dest(idx, pos, start.astype(F32).reshape(-1, 1))
    xs = _dispatch(h, dest)
    items = _work_items(counts, n * TOP_K, TM_EXPERT)
    ys = _experts(xs, items, p['moe_w1'], p['moe_w3'], p['moe_w2'], layer)
    return _combine(x, h, ys, dest, wts.T, mod,
                    p['moe_shared_w1'][layer].astype(BF16), p['moe_shared_w3'][layer].astype(BF16),
                    p['moe_shared_w2'][layer].astype(BF16), p['final_g'], cmap_c, final)


def kernel(x_prompt, x_sample, c, state_ssm, cache_k, cache_v, c_ctx, norm1_g, norm2_g, ada_w, ada_b, ssm_lam_re, ssm_lam_im, ssm_log_dt, ssm_b_re, ssm_b_im, ssm_c_re, ssm_c_im, ssm_d, ssm_glu_w, attn_w_qkv, attn_lam_q, attn_lam_k, attn_subln_g, attn_w_o, moe_router_w, moe_router_b, moe_w1, moe_w3, moe_w2, moe_shared_w1, moe_shared_w3, moe_shared_w2, final_g):
    p = dict(norm2_g=norm2_g, moe_router_w=moe_router_w, moe_router_b=moe_router_b,
             moe_w1=moe_w1, moe_w3=moe_w3, moe_w2=moe_w2, moe_shared_w1=moe_shared_w1,
             moe_shared_w3=moe_shared_w3, moe_shared_w2=moe_shared_w2, final_g=final_g)
    b_ctx, l_ctx, d = x_prompt.shape
    b_lat, l_lat, _ = x_sample.shape
    n_ctx, n_lat = b_ctx * l_ctx, b_lat * l_lat
    g = d // SSM_GROUP
    pdim = ssm_lam_re.shape[-1]
    nh = d // V_DIM
    assert 2 * pdim == LANES and CHUNK * SSM_GROUP == 2 * LANES
    assert n_ctx % TM_DENSE == 0 and l_lat % TM_DENSE == 0 and b_lat % 8 == 0 and b_ctx % 8 == 0
    nc_lat, nc_ctx = l_lat // CHUNK, l_ctx // CHUNK
    assert (b_lat * nc_lat) % (b_ctx * nc_ctx) == 0
    cmap = _cond_map(TM_DENSE, n_lat, l_lat)
    cmap_r = _cond_map(TM_ROUTER, n_lat, l_lat)
    cmap_c = _cond_map(TM_ROWS, n_lat, l_lat)

    nc = 1 + b_lat
    nc_pad = -(-nc // 8) * 8
    cond = jnp.concatenate([c_ctx[None, :], c, jnp.zeros((nc_pad - nc, d), F32)], axis=0)
    mod = _ada(cond, ada_w, ada_b)

    x = jnp.concatenate([x_sample.reshape(n_lat, d), x_prompt.reshape(n_ctx, d)], axis=0)

    u = _s5_pack(x, mod[0], norm1_g[0], cmap)
    wts = _s5_weights(ssm_lam_re[0], ssm_lam_im[0], ssm_log_dt[0], ssm_b_re[0], ssm_b_im[0],
                      ssm_c_re[0], ssm_c_im[0])
    h0 = state_ssm[:, 0].transpose(3, 0, 2, 1, 4).reshape(g, b_lat, 2 * LANES)
    y_lat, _ = _s5(u, h0, wts, 0, b_lat, nc_lat)
    y_ctx, fin = _s5(u, jnp.zeros((g, b_ctx, 2 * LANES), F32), wts, b_lat * nc_lat, b_ctx, nc_ctx)
    new_state = fin.reshape(g, b_ctx, 2, 2, pdim).transpose(1, 3, 2, 0, 4)[:, None]
    x = _glu(x, y_lat, y_ctx, mod[0], norm1_g[0], ssm_d[0], ssm_glu_w[0].astype(BF16), cmap)
    x = _moe(x, mod[0], 0, (cmap_r, cmap_c), p, final=False)

    lam_init = 0.8 - 0.6 * math.exp(-0.3 * 1)
    wqkv = attn_w_qkv[0].astype(BF16)
    q_l, k_l, v_l = _qkv(x, mod[1], norm1_g[1], wqkv, cmap, 0, n_lat, rope_l=l_lat)
    q_c, k_c, v_c, kf, vf = _qkv(x, mod[1], norm1_g[1], wqkv, cmap, n_lat, n_ctx)
    new_k = kf.reshape(b_ctx, 1, l_ctx, nh, 2, HEAD_DIM)
    new_v = vf.reshape(b_ctx, 1, l_ctx, nh, V_DIM)
    o_c = _attention(q_c.reshape(b_ctx, l_ctx, d), k_c.reshape(b_ctx, l_ctx, d), v_c.reshape(b_ctx, l_ctx, d),
                     attn_lam_q[0], attn_lam_k[0], attn_subln_g[0], lam_init)
    past = cache_k.shape[2]
    k_all = jnp.concatenate([k_l.reshape(b_lat, l_lat, d), cache_k[:, 0].reshape(b_lat, past, d).astype(BF16)], axis=1)
    v_all = jnp.concatenate([v_l.reshape(b_lat, l_lat, d), cache_v[:, 0].reshape(b_lat, past, d).astype(BF16)], axis=1)
    o_l = _attention(q_l.reshape(b_lat, l_lat, d), k_all, v_all,
                     attn_lam_q[0], attn_lam_k[0], attn_subln_g[0], lam_init)
    o = jnp.concatenate([o_l.reshape(n_lat, d), o_c.reshape(n_ctx, d)], axis=0)
    x = _oproj(x, o, mod[1], attn_w_o[0].astype(BF16), cmap)
    y = _moe(x, mod[1], 1, (cmap_r, cmap_c), p, final=True)

    return (y[n_lat:].reshape(b_ctx, l_ctx, d), y[:n_lat].reshape(b_lat, l_lat, d),
            new_state, new_k, new_v)
```

```python
import functools
import math

import jax
import jax.numpy as jnp
from jax import lax
from jax.experimental import pallas as pl
from jax.experimental.pallas import tpu as pltpu

F32 = jnp.float32
BF16 = jnp.bfloat16
I32 = jnp.int32
U32 = jnp.uint32

EPS = 1e-6
SSM_GROUP = 16
CHUNK = 16
GRID_W = 64
HEAD_DIM = 64
V_DIM = 2 * HEAD_DIM
ROPE_THETA = 10000.0
TOP_K = 8
N_EXPERT_GROUPS = 8
TOPK_GROUPS = 4
ROUTED_SCALE = 2.5
N_MOD = 6

LANES = 128
VMEM_LIMIT = 48 * 1024 * 1024

TM_DENSE = 512
TM_ROUTER = 256
TM_EXPERT = 512
TM_ROWS = 256
TQ_ATTN = 256
DMA_PRIORITIES = 2


def _params(sem):
    return pltpu.CompilerParams(dimension_semantics=sem, vmem_limit_bytes=VMEM_LIMIT)


def _normmod(x, g, sc, sh):
    ms = jnp.mean(x * x, axis=-1, keepdims=True)
    return (x * lax.rsqrt(ms + EPS) * g) * (1.0 + sc) + sh


def _cond_map(tm, n_lat, dec_seq):
    t_lat = n_lat // tm
    per = dec_seq // tm

    def f(i):
        return jnp.where(i < t_lat, 1 + i // per, 0)
    return f


def _ada_kernel(c_ref, w_ref, b_ref, o_ref):
    c = c_ref[...]
    s = (c * jax.nn.sigmoid(c)).astype(BF16)
    o_ref[0] = jnp.dot(s, w_ref[0].astype(BF16), preferred_element_type=F32) + b_ref[0]


def _ada(cond, ada_w, ada_b):
    depth, d, n6 = ada_w.shape
    nc = cond.shape[0]
    tn = 1536
    out = pl.pallas_call(
        _ada_kernel,
        grid=(depth, n6 // tn),
        in_specs=[pl.BlockSpec((nc, d), lambda l, j: (0, 0)),
                  pl.BlockSpec((1, d, tn), lambda l, j: (l, 0, j)),
                  pl.BlockSpec((1, 1, tn), lambda l, j: (l, 0, j))],
        out_specs=pl.BlockSpec((1, nc, tn), lambda l, j: (l, 0, j)),
        out_shape=jax.ShapeDtypeStruct((depth, nc, n6), F32),
        compiler_params=_params(("arbitrary", "arbitrary")),
        name="ada",
    )(cond, ada_w, ada_b.reshape(depth, 1, n6))
    return out.reshape(depth, nc, N_MOD, d)


GROUPS_PER_SLAB = LANES // SSM_GROUP
HALF_CHUNK = CHUNK // 2


def _chunk_perm():
    r = jnp.arange(HALF_CHUNK * LANES)
    t8, g8, c = r // LANES, (r // SSM_GROUP) % GROUPS_PER_SLAB, r % SSM_GROUP
    col = g8 * LANES + t8 * SSM_GROUP + c
    return (col[:, None] == r[None, :]).astype(BF16)


def _pack_kernel(x_ref, mod_ref, g_ref, p_ref, o_ref, h_s):
    m = mod_ref[0]
    h = _normmod(x_ref[...], g_ref[...], m[1:2], m[0:1])
    tm, d = h.shape
    nct = tm // CHUNK
    for j in range(d // LANES):
        h_s[j] = h[:, j * LANES:(j + 1) * LANES]
    slabs = []
    for j in range(d // LANES):
        for hf in range(2):
            slabs.append(jnp.concatenate(
                [h_s[j, pl.ds(hf * HALF_CHUNK + t8, nct, stride=CHUNK), :] for t8 in range(HALF_CHUNK)], axis=1))
    lhs = jnp.concatenate(slabs, axis=0).astype(BF16)
    out = jnp.dot(lhs, p_ref[...], preferred_element_type=F32)
    kw = CHUNK * SSM_GROUP
    for j in range(d // LANES):
        for hf in range(2):
            blk = out[(2 * j + hf) * nct:(2 * j + hf + 1) * nct]
            for g8 in range(GROUPS_PER_SLAB):
                col = (j * GROUPS_PER_SLAB + g8) * kw + hf * LANES
                o_ref[:, col:col + LANES] = blk[:, g8 * LANES:(g8 + 1) * LANES]


def _s5_pack(x, mod, g, cmap):
    n, d = x.shape
    tm = TM_DENSE
    return pl.pallas_call(
        _pack_kernel,
        grid=(n // tm,),
        in_specs=[pl.BlockSpec((tm, d), lambda i: (i, 0)),
                  pl.BlockSpec((1, N_MOD, d), lambda i: (cmap(i), 0, 0)),
                  pl.BlockSpec((1, d), lambda i: (0, 0)),
                  pl.BlockSpec((HALF_CHUNK * LANES, HALF_CHUNK * LANES), lambda i: (0, 0))],
        out_specs=pl.BlockSpec((tm // CHUNK, CHUNK * d), lambda i: (i, 0)),
        out_shape=jax.ShapeDtypeStruct((n // CHUNK, CHUNK * d), F32),
        scratch_shapes=[pltpu.VMEM((d // LANES, tm, LANES), F32)],
        compiler_params=_params(("arbitrary",)),
        name="s5_pack",
    )(x, mod, g.reshape(1, d), _chunk_perm())


def _s5_weights(lam_re, lam_im, log_dt, b_re, b_im, c_re, c_im):
    hi = lax.Precision.HIGHEST
    t = CHUNK
    w_in, w_out, toep, a16 = [], [], [], []
    for dr in range(2):
        dt = jnp.exp(log_dt[dr])[:, None]
        lr, li = lam_re[dr], lam_im[dr]
        mag = jnp.exp(lr * dt)
        a_re = mag * jnp.cos(li * dt)
        a_im = mag * jnp.sin(li * dt)
        den = lr * lr + li * li
        f_re = ((a_re - 1.0) * lr + a_im * li) / den
        f_im = (a_im * lr - (a_re - 1.0) * li) / den
        bb_re = f_re[..., None] * b_re[dr] - f_im[..., None] * b_im[dr]
        bb_im = f_re[..., None] * b_im[dr] + f_im[..., None] * b_re[dr]
        k = jnp.arange(t + 1, dtype=F32)[:, None, None]
        pm = jnp.exp(k * (lr * dt)[None])
        pk_re = pm * jnp.cos(k * (li * dt)[None])
        pk_im = pm * jnp.sin(k * (li * dt)[None])
        ks = jnp.arange(t)
        sel = (t - 1 - ks) if dr == 0 else ks
        pr, pi = pk_re[sel], pk_im[sel]
        wi_re = pr[..., None] * bb_re[None] - pi[..., None] * bb_im[None]
        wi_im = pr[..., None] * bb_im[None] + pi[..., None] * bb_re[None]
        g_, p_, c_ = bb_re.shape
        w_in.append((wi_re.transpose(1, 0, 3, 2).reshape(g_, t * c_, p_),
                     wi_im.transpose(1, 0, 3, 2).reshape(g_, t * c_, p_)))
        sel = (ks + 1) if dr == 0 else (t - ks)
        pr, pi = pk_re[sel], pk_im[sel]
        ca_re = c_re[dr][None] * pr[:, :, None, :] - c_im[dr][None] * pi[:, :, None, :]
        ca_im = c_re[dr][None] * pi[:, :, None, :] + c_im[dr][None] * pr[:, :, None, :]
        w_out.append((ca_re.transpose(1, 3, 0, 2).reshape(g_, p_, t * c_),
                      (-ca_im).transpose(1, 3, 0, 2).reshape(g_, p_, t * c_)))
        pr, pi = pk_re[:t], pk_im[:t]
        cd_re = c_re[dr][None] * pr[:, :, None, :] - c_im[dr][None] * pi[:, :, None, :]
        cd_im = c_re[dr][None] * pi[:, :, None, :] + c_im[dr][None] * pr[:, :, None, :]
        kk = jnp.einsum('kgop,gpi->kgoi', jnp.concatenate([cd_re, -cd_im], axis=3),
                        jnp.concatenate([bb_re, bb_im], axis=1), precision=hi)
        s_i = ks[:, None]
        t_i = ks[None, :]
        delta = (t_i - s_i) if dr == 0 else (s_i - t_i)
        kd = jnp.where((delta >= 0)[:, :, None, None, None], kk[jnp.clip(delta, 0, t - 1)], 0.0)
        toep.append(kd.transpose(2, 0, 4, 1, 3).reshape(g_, t * c_, t * c_))
        a16.append((pk_re[t], pk_im[t]))
    w_in_all = jnp.concatenate([w_in[0][0], w_in[1][0], w_in[0][1], w_in[1][1]], axis=2)
    w_out_all = jnp.concatenate([w_out[0][0], w_out[1][0], w_out[0][1], w_out[1][1]], axis=1)
    toep_all = toep[0] + toep[1]
    a16_all = jnp.stack([jnp.concatenate([a16[0][0], a16[1][0]], axis=1),
                         jnp.concatenate([a16[0][1], a16[1][1]], axis=1)], axis=1)
    return w_in_all.astype(BF16), toep_all.astype(BF16), w_out_all.astype(BF16), a16_all


def _s5_kernel(u_hbm, win_ref, wtp_ref, wout_ref, a16_ref, h0_ref, y_hbm, fin_ref,
               ubuf, ybuf, v_s, sf_s, sb_s, sem_in, sem_out, *, nc, nb, row0):
    g = pl.program_id(0)
    ng = pl.num_programs(0)
    kw = ubuf.shape[3]
    slot = g % 2

    def in_copy(grp, s, b):
        return pltpu.make_async_copy(u_hbm.at[pl.ds(row0 + b * nc, nc), pl.ds(grp * kw, kw)],
                                     ubuf.at[s, :, b, :], sem_in.at[s])

    def out_copy(grp, s, b):
        return pltpu.make_async_copy(ybuf.at[s, :, b, :],
                                     y_hbm.at[pl.ds(b * nc, nc), pl.ds(grp * kw, kw)], sem_out.at[s])

    @pl.when(g == 0)
    def _():
        for b in range(nb):
            in_copy(g, slot, b).start()

    @pl.when(g + 1 < ng)
    def _():
        for b in range(nb):
            in_copy(g + 1, 1 - slot, b).start()

    for b in range(nb):
        in_copy(g, slot, b).wait()
    u = ubuf[slot].reshape(nc * nb, kw).astype(BF16)
    v_s[...] = jnp.dot(u, win_ref[0], preferred_element_type=F32)
    a = a16_ref[0]
    ar, ai = a[0:1], a[1:2]
    fwd = (lax.broadcasted_iota(I32, (nb, 2 * LANES), 1) % LANES) < (LANES // 2)

    def step(i, carry):
        sr, si = carry
        st = jnp.concatenate([sr, si], axis=1)
        rf = pl.multiple_of(i * nb, nb)
        rb = pl.multiple_of((nc - 1 - i) * nb, nb)
        sf_s[pl.ds(rf, nb), :] = st
        sb_s[pl.ds(rb, nb), :] = st
        v = jnp.where(fwd, v_s[pl.ds(rf, nb), :], v_s[pl.ds(rb, nb), :])
        return ar * sr - ai * si + v[:, :LANES], ar * si + ai * sr + v[:, LANES:]

    h0 = h0_ref[0]
    sr, si = lax.fori_loop(0, nc, step, (h0[:, :LANES], h0[:, LANES:]))
    fin_ref[0] = jnp.concatenate([sr, si], axis=1)
    fwd_all = (lax.broadcasted_iota(I32, (nc * nb, 2 * LANES), 1) % LANES) < (LANES // 2)
    hs = jnp.where(fwd_all, sf_s[...], sb_s[...]).astype(BF16)
    y = (jnp.dot(u, wtp_ref[0], preferred_element_type=F32)
         + jnp.dot(hs, wout_ref[0], preferred_element_type=F32))

    @pl.when(g >= 2)
    def _():
        for b in range(nb):
            out_copy(g - 2, slot, b).wait()

    ybuf[slot] = y.reshape(nc, nb, kw)
    for b in range(nb):
        out_copy(g, slot, b).start()

    @pl.when(g == ng - 1)
    def _():
        if ng > 1:
            for b in range(nb):
                out_copy(g - 1, 1 - slot, b).wait()
        for b in range(nb):
            out_copy(g, slot, b).wait()


def _s5(u, h0, wts, row0, nb, nc):
    w_in, w_tp, w_out, a16 = wts
    r = nb * nc
    kw = CHUNK * SSM_GROUP
    g = u.shape[1] // kw
    return pl.pallas_call(
        functools.partial(_s5_kernel, nc=nc, nb=nb, row0=row0),
        grid=(g,),
        in_specs=[pl.BlockSpec(memory_space=pl.ANY),
                  pl.BlockSpec((1, kw, kw), lambda j: (j, 0, 0)),
                  pl.BlockSpec((1, kw, kw), lambda j: (j, 0, 0)),
                  pl.BlockSpec((1, kw, kw), lambda j: (j, 0, 0)),
                  pl.BlockSpec((1, 2, LANES), lambda j: (j, 0, 0)),
                  pl.BlockSpec((1, nb, kw), lambda j: (j, 0, 0))],
        out_specs=[pl.BlockSpec(memory_space=pl.ANY),
                   pl.BlockSpec((1, nb, kw), lambda j: (j, 0, 0))],
        out_shape=[jax.ShapeDtypeStruct((r, u.shape[1]), F32),
                   jax.ShapeDtypeStruct((g, nb, kw), F32)],
        scratch_shapes=[pltpu.VMEM((2, nc, nb, kw), F32), pltpu.VMEM((2, nc, nb, kw), F32),
                        pltpu.VMEM((r, kw), F32), pltpu.VMEM((r, kw), F32), pltpu.VMEM((r, kw), F32),
                        pltpu.SemaphoreType.DMA((2,)), pltpu.SemaphoreType.DMA((2,))],
        compiler_params=_params(("arbitrary",)),
        name="s5_scan",
    )(u, w_in, w_tp, w_out, a16, h0)


def _glu_kernel(x_ref, ya_ref, yb_ref, mod_ref, g_ref, d_ref, pt_ref, w_ref, o_ref, y_s, *, tiles_a):
    tm, dd = x_ref.shape
    nct = tm // CHUNK
    kw = CHUNK * SSM_GROUP
    from_a = pl.program_id(0) < tiles_a
    slabs = []
    for j in range(dd // LANES):
        for hf in range(2):
            cols = [(j * GROUPS_PER_SLAB + g8) * kw + hf * LANES for g8 in range(GROUPS_PER_SLAB)]
            slabs.append(jnp.concatenate(
                [jnp.where(from_a, ya_ref[:, c0:c0 + LANES], yb_ref[:, c0:c0 + LANES]) for c0 in cols], axis=1))
    lhs = jnp.concatenate(slabs, axis=0)
    hi = lhs.astype(BF16)
    lo = (lhs - hi.astype(F32)).astype(BF16)
    pt = pt_ref[...]
    nat = jnp.dot(hi, pt, preferred_element_type=F32) + jnp.dot(lo, pt, preferred_element_type=F32)
    for j in range(dd // LANES):
        for hf in range(2):
            blk = nat[(2 * j + hf) * nct:(2 * j + hf + 1) * nct]
            for t8 in range(HALF_CHUNK):
                y_s[j, pl.ds(hf * HALF_CHUNK + t8, nct, stride=CHUNK), :] = blk[:, t8 * LANES:(t8 + 1) * LANES]
    m = mod_ref[0]
    x = x_ref[...]
    h = _normmod(x, g_ref[...], m[1:2], m[0:1])
    y = d_ref[...] * h + jnp.concatenate([y_s[j] for j in range(dd // LANES)], axis=1)
    gl = jax.nn.gelu(y).astype(BF16)
    z = jnp.dot(gl, w_ref[...], preferred_element_type=F32)
    o_ref[...] = x + m[2:3] * (z[:, :dd] * jax.nn.sigmoid(z[:, dd:]))


def _glu(x, y_a, y_b, mod, g, dskip, w, cmap):
    n, d = x.shape
    tm = TM_DENSE
    nct = tm // CHUNK
    tiles_a = y_a.shape[0] // nct
    last_a = tiles_a - 1
    return pl.pallas_call(
        functools.partial(_glu_kernel, tiles_a=tiles_a),
        grid=(n // tm,),
        in_specs=[pl.BlockSpec((tm, d), lambda i: (i, 0)),
                  pl.BlockSpec((nct, CHUNK * d), lambda i: (jnp.minimum(i, last_a), 0)),
                  pl.BlockSpec((nct, CHUNK * d), lambda i: (jnp.maximum(i - tiles_a, 0), 0)),
                  pl.BlockSpec((1, N_MOD, d), lambda i: (cmap(i), 0, 0)),
                  pl.BlockSpec((1, d), lambda i: (0, 0)),
                  pl.BlockSpec((1, d), lambda i: (0, 0)),
                  pl.BlockSpec((HALF_CHUNK * LANES, HALF_CHUNK * LANES), lambda i: (0, 0)),
                  pl.BlockSpec((d, 2 * d), lambda i: (0, 0))],
        out_specs=pl.BlockSpec((tm, d), lambda i: (i, 0)),
        out_shape=jax.ShapeDtypeStruct((n, d), F32),
        scratch_shapes=[pltpu.VMEM((d // LANES, tm, LANES), F32)],
        compiler_params=_params(("arbitrary",)),
        name="s5_glu",
    )(x, y_a, y_b, mod, g.reshape(1, d), dskip.reshape(1, d), _chunk_perm().T, w)


def _rope_tables(l):
    r = HEAD_DIM // 2
    half = r // 2
    freq = ROPE_THETA ** (-jnp.arange(half, dtype=F32) / half)
    pos = jnp.arange(l)
    row = (pos // GRID_W).astype(F32)
    col = (pos % GRID_W).astype(F32)
    d = jnp.arange(LANES) % HEAD_DIM
    p = jnp.where((d < r)[None, :], row[:, None], col[:, None])
    ang = p * freq[d % half][None, :]
    sign = jnp.where((d % r) < half, -1.0, 1.0)[None, :]
    return jnp.cos(ang), jnp.sin(ang) * sign


def _qkv_kernel(*refs, rope):
    if rope:
        x_ref, mod_ref, g_ref, w_ref, cos_ref, sin_ref, q_ref, k_ref, v_ref = refs
    else:
        x_ref, mod_ref, g_ref, w_ref, q_ref, k_ref, v_ref, kf_ref, vf_ref = refs
    m = mod_ref[0]
    h = _normmod(x_ref[...], g_ref[...], m[1:2], m[0:1]).astype(BF16)
    qkv = jnp.dot(h, w_ref[...], preferred_element_type=F32)
    d = x_ref.shape[1]
    q = qkv[:, :d] * (HEAD_DIM ** -0.5 * math.log2(math.e))
    k = qkv[:, d:2 * d]
    v = qkv[:, 2 * d:]
    v_ref[...] = v.astype(BF16)
    if not rope:
        q_ref[...] = q.astype(BF16)
        k_ref[...] = k.astype(BF16)
        kf_ref[...] = k
        vf_ref[...] = v
        return
    cos = cos_ref[...]
    sin = sin_ref[...]
    half = HEAD_DIM // 4
    lane = lax.broadcasted_iota(I32, cos.shape, 1)
    up_sel = (lane % (2 * half)) < half
    for src, dst in ((q, q_ref), (k, k_ref)):
        for j in range(d // LANES):
            blk = src[:, j * LANES:(j + 1) * LANES]
            up = pltpu.roll(blk, LANES - half, 1)
            dn = pltpu.roll(blk, half, 1)
            dst[:, j * LANES:(j + 1) * LANES] = (blk * cos + jnp.where(up_sel, up, dn) * sin).astype(BF16)


def _qkv(x, mod, g, w, cmap, row0, nrows, rope_l=None):
    n, d = x.shape
    tm = TM_DENSE
    t0 = row0 // tm
    rope = rope_l is not None
    in_specs = [pl.BlockSpec((tm, d), lambda i: (i + t0, 0)),
                pl.BlockSpec((1, N_MOD, d), lambda i: (cmap(i + t0), 0, 0)),
                pl.BlockSpec((1, d), lambda i: (0, 0)),
                pl.BlockSpec((d, 3 * d), lambda i: (0, 0))]
    args = [x, mod, g.reshape(1, d), w]
    row_spec = pl.BlockSpec((tm, d), lambda i: (i, 0))
    out_specs = [row_spec] * 3
    out_shape = [jax.ShapeDtypeStruct((nrows, d), BF16)] * 3
    if rope:
        per = rope_l // tm
        cos, sin = _rope_tables(rope_l)
        in_specs += [pl.BlockSpec((tm, LANES), lambda i: (i % per, 0))] * 2
        args += [cos, sin]
    else:
        out_specs += [row_spec] * 2
        out_shape += [jax.ShapeDtypeStruct((nrows, d), F32)] * 2
    return pl.pallas_call(
        functools.partial(_qkv_kernel, rope=rope),
        grid=(nrows // tm,),
        in_specs=in_specs,
        out_specs=out_specs,
        out_shape=out_shape,
        compiler_params=_params(("arbitrary",)),
        name="qkv_rope" if rope else "qkv",
    )(*args)


def _attn_kernel(q_ref, kt_ref, v_ref, lq_ref, lk_ref, g_ref, o_ref, *, lam_init):
    q = q_ref[0]
    tq = q.shape[0]
    lane = lax.broadcasted_iota(I32, q.shape, 1)
    zero = jnp.zeros_like(q)
    qq = jnp.concatenate([jnp.where(lane < HEAD_DIM, q, zero), jnp.where(lane >= HEAD_DIM, q, zero)], axis=0)
    s = jnp.dot(qq, kt_ref[0, 0], preferred_element_type=F32)
    p = jnp.exp2(s - jnp.max(s, axis=1, keepdims=True))
    l = jnp.sum(p, axis=1, keepdims=True)
    lqk = lq_ref[...] * lk_ref[...]
    lam = (jnp.exp(jnp.sum(lqk[0:1], axis=1, keepdims=True))
           - jnp.exp(jnp.sum(lqk[1:2], axis=1, keepdims=True)) + lam_init)
    w = p[:tq] * (1.0 / l[:tq]) - p[tq:] * (lam / l[tq:])
    o = jnp.dot(w.astype(BF16), v_ref[0], preferred_element_type=F32)
    o = o * lax.rsqrt(jnp.mean(o * o, axis=1, keepdims=True) + EPS) * g_ref[...]
    o_ref[0] = (o * (1.0 - lam_init)).astype(o_ref.dtype)


def _attention(q, k, v, lam_q, lam_k, subln_g, lam_init):
    nb, l, d = q.shape
    lk = k.shape[1]
    nh = d // V_DIM
    kt = k.reshape(nb, lk, nh, V_DIM).transpose(0, 2, 3, 1)
    tq = min(TQ_ATTN, l)
    return pl.pallas_call(
        functools.partial(_attn_kernel, lam_init=lam_init),
        grid=(nb, nh, l // tq),
        in_specs=[pl.BlockSpec((1, tq, V_DIM), lambda b, h, i: (b, i, h)),
                  pl.BlockSpec((1, 1, V_DIM, lk), lambda b, h, i: (b, h, 0, 0)),
                  pl.BlockSpec((1, lk, V_DIM), lambda b, h, i: (b, 0, h)),
                  pl.BlockSpec((2, HEAD_DIM), lambda b, h, i: (0, 0)),
                  pl.BlockSpec((2, HEAD_DIM), lambda b, h, i: (0, 0)),
                  pl.BlockSpec((1, V_DIM), lambda b, h, i: (0, 0))],
        out_specs=pl.BlockSpec((1, tq, V_DIM), lambda b, h, i: (b, i, h)),
        out_shape=jax.ShapeDtypeStruct((nb, l, d), BF16),
        compiler_params=_params(("arbitrary", "arbitrary", "arbitrary")),
        name="diff_attn",
    )(q, kt, v, lam_q, lam_k, subln_g.reshape(1, V_DIM))


def _oproj_kernel(x_ref, o_ref, mod_ref, w_ref, y_ref):
    m = mod_ref[0]
    y_ref[...] = x_ref[...] + m[2:3] * jnp.dot(o_ref[...], w_ref[...], preferred_element_type=F32)


def _oproj(x, o, mod, w, cmap):
    n, d = x.shape
    tm = TM_DENSE
    return pl.pallas_call(
        _oproj_kernel,
        grid=(n // tm,),
        in_specs=[pl.BlockSpec((tm, d), lambda i: (i, 0)),
                  pl.BlockSpec((tm, d), lambda i: (i, 0)),
                  pl.BlockSpec((1, N_MOD, d), lambda i: (cmap(i), 0, 0)),
                  pl.BlockSpec((d, d), lambda i: (0, 0))],
        out_specs=pl.BlockSpec((tm, d), lambda i: (i, 0)),
        out_shape=jax.ShapeDtypeStruct((n, d), F32),
        compiler_params=_params(("arbitrary",)),
        name="attn_out",
    )(x, o, mod, w)


def _pack_rows(h):
    half = h.shape[1] // 2
    bits = lax.bitcast_convert_type(h.astype(BF16).astype(F32), U32)
    return bits[:, half:] | (bits[:, :half] >> 16)


def _unpack_rows(w):
    lo = lax.bitcast_convert_type(w << 16, F32)
    hi = lax.bitcast_convert_type(w & jnp.uint32(0xFFFF0000), F32)
    return lo, hi


def _stack_rows(rows, width):
    ri = lax.broadcasted_iota(I32, (len(rows), width), 0)
    out = jnp.zeros((len(rows), width), rows[0].dtype)
    for j, r in enumerate(rows):
        out = jnp.where(ri == j, r, out)
    return out


def _router_kernel(x_ref, mod_ref, g_ref, wh_ref, wl_ref, b_ref, h_ref, idx_ref, w_ref, pos_ref, cnt_ref, cnt_s):
    step = pl.program_id(0)

    @pl.when(step == 0)
    def _():
        cnt_s[...] = jnp.zeros_like(cnt_s)

    m = mod_ref[0]
    h = _normmod(x_ref[...], g_ref[...], m[4:5], m[3:4])
    h_ref[...] = _pack_rows(h)
    tr = h.shape[0]
    ne = wh_ref.shape[0]
    per = ne // N_EXPERT_GROUPS
    hh = h.astype(BF16)
    hl = (h - hh.astype(F32)).astype(BF16)
    nt = (((1,), (1,)), ((), ()))
    wh = wh_ref[...]
    logits = (lax.dot_general(wh, hh, nt, preferred_element_type=F32)
              + lax.dot_general(wl_ref[...], hh, nt, preferred_element_type=F32)
              + lax.dot_general(wh, hl, nt, preferred_element_type=F32))
    scores = jax.nn.sigmoid(logits)
    biased = scores + b_ref[...]
    neg = -jnp.inf
    ri = lax.broadcasted_iota(I32, (per, tr), 0).astype(F32)
    blocks, gscores = [], []
    for j in range(N_EXPERT_GROUPS):
        blk = biased[j * per:(j + 1) * per, :]
        m1 = jnp.max(blk, axis=0, keepdims=True)
        f1 = jnp.min(jnp.where(blk == m1, ri, float(per)), axis=0, keepdims=True)
        m2 = jnp.max(jnp.where(ri == f1, neg, blk), axis=0, keepdims=True)
        blocks.append(blk)
        gscores.append(m1 + m2)
    gs = _stack_rows(gscores, tr)
    gi = lax.broadcasted_iota(I32, gs.shape, 0).astype(F32)
    gsel = jnp.zeros(gs.shape, F32)
    for _ in range(TOPK_GROUPS):
        mm = jnp.max(gs, axis=0, keepdims=True)
        pick = gi == jnp.min(jnp.where(gs == mm, gi, float(N_EXPERT_GROUPS)), axis=0, keepdims=True)
        gsel = jnp.where(pick, 1.0, gsel)
        gs = jnp.where(pick, neg, gs)
    masked = jnp.concatenate(
        [jnp.where(gsel[j:j + 1, :] > 0.0, blocks[j], neg) for j in range(N_EXPERT_GROUPS)], axis=0)
    ei = lax.broadcasted_iota(I32, masked.shape, 0).astype(F32)
    picks, idxs, ws = [], [], []
    for _ in range(TOP_K):
        mm = jnp.max(masked, axis=0, keepdims=True)
        ff = jnp.min(jnp.where(masked == mm, ei, float(ne)), axis=0, keepdims=True)
        pick = ei == ff
        picks.append(pick)
        idxs.append(ff)
        ws.append(jnp.sum(jnp.where(pick, scores, 0.0), axis=0, keepdims=True))
        masked = jnp.where(pick, neg, masked)
    wsum = ws[0]
    for wk in ws[1:]:
        wsum = wsum + wk
    w_ref[...] = _stack_rows([wk / wsum * ROUTED_SCALE for wk in ws], tr)
    idx_ref[...] = _stack_rows(idxs, tr).astype(I32)
    sel = jnp.zeros(masked.shape, F32)
    for pick in picks:
        sel = sel + jnp.where(pick, 1.0, 0.0)
    earlier = (lax.broadcasted_iota(I32, (tr, tr), 0) < lax.broadcasted_iota(I32, (tr, tr), 1))
    prefix = jnp.dot(sel.astype(BF16), jnp.where(earlier, 1.0, 0.0).astype(BF16), preferred_element_type=F32)
    base = prefix + cnt_s[...]
    pos_ref[...] = _stack_rows(
        [jnp.sum(jnp.where(pick, base, 0.0), axis=0, keepdims=True) for pick in picks], tr).astype(I32)
    cnt_s[...] = cnt_s[...] + jnp.sum(sel, axis=1, keepdims=True)
    cnt_ref[...] = cnt_s[...]


def _router(x, mod, g, w_t_hi, w_t_lo, bias, cmap):
    n, d = x.shape
    ne = w_t_hi.shape[0]
    tr = TM_ROUTER
    tok_spec = pl.BlockSpec((TOP_K, tr), lambda i: (0, i))
    return pl.pallas_call(
        _router_kernel,
        grid=(n // tr,),
        in_specs=[pl.BlockSpec((tr, d), lambda i: (i, 0)),
                  pl.BlockSpec((1, N_MOD, d), lambda i: (cmap(i), 0, 0)),
                  pl.BlockSpec((1, d), lambda i: (0, 0)),
                  pl.BlockSpec((ne, d), lambda i: (0, 0)),
                  pl.BlockSpec((ne, d), lambda i: (0, 0)),
                  pl.BlockSpec((ne, 1), lambda i: (0, 0))],
        out_specs=[pl.BlockSpec((tr, d // 2), lambda i: (i, 0)), tok_spec, tok_spec, tok_spec,
                   pl.BlockSpec((ne, 1), lambda i: (0, 0))],
        out_shape=[jax.ShapeDtypeStruct((n, d // 2), U32),
                   jax.ShapeDtypeStruct((TOP_K, n), I32),
                   jax.ShapeDtypeStruct((TOP_K, n), F32),
                   jax.ShapeDtypeStruct((TOP_K, n), I32),
                   jax.ShapeDtypeStruct((ne, 1), F32)],
        scratch_shapes=[pltpu.VMEM((ne, 1), F32)],
        compiler_params=_params(("arbitrary",)),
        name="moe_router",
    )(x, mod, g.reshape(1, d), w_t_hi, w_t_lo, bias.reshape(ne, 1))


def _dest_kernel(idx_ref, pos_ref, start_ref, o_ref):
    idx = idx_ref[...]
    ei = lax.broadcasted_iota(I32, (start_ref.shape[0], idx.shape[1]), 0)
    start = start_ref[...]
    rows = [jnp.sum(jnp.where(ei == idx[k:k + 1], start, 0.0), axis=0, keepdims=True) for k in range(TOP_K)]
    o_ref[...] = _stack_rows(rows, idx.shape[1]).astype(I32) + pos_ref[...]


def _dest(idx, pos, start):
    n = idx.shape[1]
    ne = start.shape[0]
    tr = 512
    tok_spec = pl.BlockSpec((TOP_K, tr), lambda i: (0, i))
    return pl.pallas_call(
        _dest_kernel,
        grid=(n // tr,),
        in_specs=[tok_spec, tok_spec, pl.BlockSpec((ne, 1), lambda i: (0, 0))],
        out_specs=tok_spec,
        out_shape=jax.ShapeDtypeStruct((TOP_K, n), I32),
        compiler_params=_params(("arbitrary",)),
        name="moe_dest",
    )(idx, pos, start)


def _dispatch_kernel(dest_ref, h_ref, xs_ref, sem):
    tm = h_ref.shape[0]

    def issue(t, carry):
        for k in range(TOP_K):
            pltpu.make_async_copy(h_ref.at[pl.ds(t, 1), :], xs_ref.at[pl.ds(dest_ref[k, t], 1), :],
                                  sem).start(priority=k % DMA_PRIORITIES)
        return carry

    lax.fori_loop(0, tm, issue, 0)
    for _ in range(TOP_K):
        pltpu.make_async_copy(h_ref, xs_ref.at[pl.ds(0, tm), :], sem).wait()


def _dispatch(h, dest):
    n, d = h.shape
    tm = TM_ROWS
    return pl.pallas_call(
        _dispatch_kernel,
        grid=(n // tm,),
        in_specs=[pl.BlockSpec((TOP_K, tm), lambda i: (0, i), memory_space=pltpu.SMEM),
                  pl.BlockSpec((tm, d), lambda i: (i, 0))],
        out_specs=pl.BlockSpec(memory_space=pl.ANY),
        out_shape=jax.ShapeDtypeStruct((n * TOP_K, d), U32),
        scratch_shapes=[pltpu.SemaphoreType.DMA(())],
        compiler_params=_params(("arbitrary",)),
        name="moe_dispatch",
    )(dest, h)


def _expert_kernel(tile_ref, exp_ref, lo_ref, hi_ref, newe_ref, newt_ref, last_ref,
                   xs_ref, w1_ref, w3_ref, w2_ref, ys_ref, w1_s, w3_s, w2_s, acc_s):
    i = pl.program_id(0)

    @pl.when(newe_ref[i] == 1)
    def _():
        w1_s[...] = w1_ref[0, 0].astype(BF16)
        w3_s[...] = w3_ref[0, 0].astype(BF16)
        w2_s[...] = w2_ref[0, 0].astype(BF16)

    tm = xs_ref.shape[0]
    lo, hi = lo_ref[i], hi_ref[i]

    @pl.when(hi > lo)
    def _():
        x_lo, x_hi = _unpack_rows(xs_ref[...])
        x = jnp.concatenate([x_lo, x_hi], axis=1).astype(BF16)
        a = jnp.dot(x, w1_s[...], preferred_element_type=F32)
        b = jnp.dot(x, w3_s[...], preferred_element_type=F32)
        hb = (a * jax.nn.sigmoid(a) * b).astype(BF16)
        y = jnp.dot(hb, w2_s[...], preferred_element_type=F32)
        row = tile_ref[i] * tm + lax.broadcasted_iota(I32, (tm, 1), 0)
        y = jnp.where((row >= lo) & (row < hi), y, 0.0)

        @pl.when(newt_ref[i] == 1)
        def _():
            acc_s[...] = y

        @pl.when(newt_ref[i] == 0)
        def _():
            acc_s[...] = acc_s[...] + y

    @pl.when(last_ref[i] == 1)
    def _():
        ys_ref[...] = _pack_rows(acc_s[...])


def _work_items(counts, n_rows, tm):
    ne = counts.shape[0]
    n_tiles = n_rows // tm
    n_items = n_tiles + ne - 1
    end = jnp.cumsum(counts)
    start = end - counts
    first_t = start // tm
    n_e = jnp.where(counts > 0, (end - 1) // tm - first_t + 1, 0)
    item_end = jnp.cumsum(n_e)
    item_off = item_end - n_e
    it = jnp.arange(n_items, dtype=I32)
    total = item_end[-1]
    valid = it < total
    last_item = jnp.minimum(it, total - 1)
    e = jnp.minimum(jnp.sum((item_end[None, :] <= last_item[:, None]).astype(I32), axis=1), ne - 1).astype(I32)
    tile = jnp.where(valid, first_t[e] + it - item_off[e], n_tiles - 1).astype(I32)
    lo = jnp.where(valid, jnp.maximum(start[e], tile * tm), 0).astype(I32)
    hi = jnp.where(valid, jnp.minimum(end[e], (tile + 1) * tm), 0).astype(I32)
    prev_e = jnp.concatenate([jnp.full((1,), -1, I32), e[:-1]])
    prev_t = jnp.concatenate([jnp.full((1,), -1, I32), tile[:-1]])
    next_t = jnp.concatenate([tile[1:], jnp.full((1,), -1, I32)])
    return (tile, e, lo, hi, (e != prev_e).astype(I32), (tile != prev_t).astype(I32),
            (tile != next_t).astype(I32))


def _experts(xs, items, w1, w3, w2, layer):
    n_rows, dw = xs.shape
    d, de = w1.shape[2], w1.shape[3]
    tm = TM_EXPERT
    n_items = items[0].shape[0]
    grid_spec = pltpu.PrefetchScalarGridSpec(
        num_scalar_prefetch=len(items),
        grid=(n_items,),
        in_specs=[pl.BlockSpec((tm, dw), lambda i, t, e, *_: (t[i], 0)),
                  pl.BlockSpec((1, 1, d, de), lambda i, t, e, *_: (layer, e[i], 0, 0)),
                  pl.BlockSpec((1, 1, d, de), lambda i, t, e, *_: (layer, e[i], 0, 0)),
                  pl.BlockSpec((1, 1, de, d), lambda i, t, e, *_: (layer, e[i], 0, 0))],
        out_specs=pl.BlockSpec((tm, dw), lambda i, t, e, *_: (t[i], 0)),
        scratch_shapes=[pltpu.VMEM((d, de), BF16), pltpu.VMEM((d, de), BF16), pltpu.VMEM((de, d), BF16),
                        pltpu.VMEM((tm, d), F32)],
    )
    return pl.pallas_call(
        _expert_kernel,
        grid_spec=grid_spec,
        out_shape=jax.ShapeDtypeStruct((n_rows, dw), U32),
        compiler_params=_params(("arbitrary",)),
        name="moe_experts",
    )(*items, xs, w1, w3, w2)


def _combine_kernel(dest_ref, dest_next_ref, x_ref, h_ref, w_ref, mod_ref, s1_ref, s3_ref, s2_ref, fg_ref, ys_ref,
                    o_ref, buf, sem, *, final):
    i = pl.program_id(0)
    tm = x_ref.shape[0]
    slot = i % 2

    def issue(dref, s):
        def body(t, carry):
            for k in range(TOP_K):
                pltpu.make_async_copy(ys_ref.at[pl.ds(dref[k, t], 1), :], buf.at[s, k, pl.ds(t, 1), :],
                                      sem.at[s]).start(priority=k % DMA_PRIORITIES)
            return carry
        lax.fori_loop(0, tm, body, 0)

    @pl.when(i == 0)
    def _():
        issue(dest_ref, slot)

    @pl.when(i + 1 < pl.num_programs(0))
    def _():
        issue(dest_next_ref, 1 - slot)

    h_lo, h_hi = _unpack_rows(h_ref[...])
    hb = jnp.concatenate([h_lo, h_hi], axis=1).astype(BF16)
    a = jnp.dot(hb, s1_ref[...], preferred_element_type=F32)
    b = jnp.dot(hb, s3_ref[...], preferred_element_type=F32)
    acc = jnp.dot((a * jax.nn.sigmoid(a) * b).astype(BF16), s2_ref[...], preferred_element_type=F32)
    for k in range(TOP_K):
        pltpu.make_async_copy(ys_ref.at[pl.ds(0, tm), :], buf.at[slot, k], sem.at[slot]).wait()
    w = w_ref[...]
    half = acc.shape[1] // 2
    acc_lo, acc_hi = acc[:, :half], acc[:, half:]
    for k in range(TOP_K):
        y_lo, y_hi = _unpack_rows(buf[slot, k])
        acc_lo = acc_lo + w[:, k:k + 1] * y_lo
        acc_hi = acc_hi + w[:, k:k + 1] * y_hi
    acc = jnp.concatenate([acc_lo, acc_hi], axis=1)
    m = mod_ref[0]
    y = x_ref[...] + m[5:6] * acc
    if final:
        y = y * lax.rsqrt(jnp.mean(y * y, axis=-1, keepdims=True) + EPS) * fg_ref[...]
    o_ref[...] = y


def _combine(x, h, ys, dest, w, mod, s1, s3, s2, final_g, cmap, final):
    n, d = x.shape
    ds_ = s1.shape[1]
    tm = TM_ROWS
    last = n // tm - 1
    return pl.pallas_call(
        functools.partial(_combine_kernel, final=final),
        grid=(n // tm,),
        in_specs=[pl.BlockSpec((TOP_K, tm), lambda i: (0, i), memory_space=pltpu.SMEM),
                  pl.BlockSpec((TOP_K, tm), lambda i: (0, jnp.minimum(i + 1, last)), memory_space=pltpu.SMEM),
                  pl.BlockSpec((tm, d), lambda i: (i, 0)),
                  pl.BlockSpec((tm, d // 2), lambda i: (i, 0)),
                  pl.BlockSpec((tm, TOP_K), lambda i: (i, 0)),
                  pl.BlockSpec((1, N_MOD, d), lambda i: (cmap(i), 0, 0)),
                  pl.BlockSpec((d, ds_), lambda i: (0, 0)),
                  pl.BlockSpec((d, ds_), lambda i: (0, 0)),
                  pl.BlockSpec((ds_, d), lambda i: (0, 0)),
                  pl.BlockSpec((1, d), lambda i: (0, 0)),
                  pl.BlockSpec(memory_space=pl.ANY)],
        out_specs=pl.BlockSpec((tm, d), lambda i: (i, 0)),
        out_shape=jax.ShapeDtypeStruct((n, d), F32),
        scratch_shapes=[pltpu.VMEM((2, TOP_K, tm, d // 2), U32), pltpu.SemaphoreType.DMA((2,))],
        compiler_params=_params(("arbitrary",)),
        name="moe_combine",
    )(dest, dest, x, h, w, mod, s1, s3, s2, final_g.reshape(1, d), ys)


def _moe(x, mod, layer, cmaps, p, final):
    cmap_r, cmap_c = cmaps
    n, d = x.shape
    rw = p['moe_router_w'][layer].T
    rw_hi = rw.astype(BF16)
    rw_lo = (rw - rw_hi.astype(F32)).astype(BF16)
    h, idx, wts, pos, cnt = _router(x, mod, p['norm2_g'][layer], rw_hi, rw_lo, p['moe_router_b'][layer], cmap_r)
    counts = cnt[:, 0].astype(I32)
    start = jnp.cumsum(counts) - counts
    dest = _dest(idx, pos, start.astype(F32).reshape(-1, 1))
    xs = _dispatch(h, dest)
    items = _work_items(counts, n * TOP_K, TM_EXPERT)
    ys = _experts(xs, items, p['moe_w1'], p['moe_w3'], p['moe_w2'], layer)
    return _combine(x, h, ys, dest, wts.T, mod,
                    p['moe_shared_w1'][layer].astype(BF16), p['moe_shared_w3'][layer].astype(BF16),
                    p['moe_shared_w2'][layer].astype(BF16), p['final_g'], cmap_c, final)


def kernel(x_prompt, x_sample, c, state_ssm, cache_k, cache_v, c_ctx, norm1_g, norm2_g, ada_w, ada_b, ssm_lam_re, ssm_lam_im, ssm_log_dt, ssm_b_re, ssm_b_im, ssm_c_re, ssm_c_im, ssm_d, ssm_glu_w, attn_w_qkv, attn_lam_q, attn_lam_k, attn_subln_g, attn_w_o, moe_router_w, moe_router_b, moe_w1, moe_w3, moe_w2, moe_shared_w1, moe_shared_w3, moe_shared_w2, final_g):
    p = dict(norm2_g=norm2_g, moe_router_w=moe_router_w, moe_router_b=moe_router_b,
             moe_w1=moe_w1, moe_w3=moe_w3, moe_w2=moe_w2, moe_shared_w1=moe_shared_w1,
             moe_shared_w3=moe_shared_w3, moe_shared_w2=moe_shared_w2, final_g=final_g)
    b_ctx, l_ctx, d = x_prompt.shape
    b_lat, l_lat, _ = x_sample.shape
    n_ctx, n_lat = b_ctx * l_ctx, b_lat * l_lat
    g = d // SSM_GROUP
    pdim = ssm_lam_re.shape[-1]
    nh = d // V_DIM
    assert 2 * pdim == LANES and CHUNK * SSM_GROUP == 2 * LANES
    assert n_ctx % TM_DENSE == 0 and l_lat % TM_DENSE == 0 and b_lat % 8 == 0 and b_ctx % 8 == 0
    nc_lat, nc_ctx = l_lat // CHUNK, l_ctx // CHUNK
    assert (b_lat * nc_lat) % (b_ctx * nc_ctx) == 0
    cmap = _cond_map(TM_DENSE, n_lat, l_lat)
    cmap_r = _cond_map(TM_ROUTER, n_lat, l_lat)
    cmap_c = _cond_map(TM_ROWS, n_lat, l_lat)

    nc = 1 + b_lat
    nc_pad = -(-nc // 8) * 8
    cond = jnp.concatenate([c_ctx[None, :], c, jnp.zeros((nc_pad - nc, d), F32)], axis=0)
    mod = _ada(cond, ada_w, ada_b)

    x = jnp.concatenate([x_sample.reshape(n_lat, d), x_prompt.reshape(n_ctx, d)], axis=0)

    u = _s5_pack(x, mod[0], norm1_g[0], cmap)
    wts = _s5_weights(ssm_lam_re[0], ssm_lam_im[0], ssm_log_dt[0], ssm_b_re[0], ssm_b_im[0],
                      ssm_c_re[0], ssm_c_im[0])
    h0 = state_ssm[:, 0].transpose(3, 0, 2, 1, 4).reshape(g, b_lat, 2 * LANES)
    y_lat, _ = _s5(u, h0, wts, 0, b_lat, nc_lat)
    y_ctx, fin = _s5(u, jnp.zeros((g, b_ctx, 2 * LANES), F32), wts, b_lat * nc_lat, b_ctx, nc_ctx)
    new_state = fin.reshape(g, b_ctx, 2, 2, pdim).transpose(1, 3, 2, 0, 4)[:, None]
    x = _glu(x, y_lat, y_ctx, mod[0], norm1_g[0], ssm_d[0], ssm_glu_w[0].astype(BF16), cmap)
    x = _moe(x, mod[0], 0, (cmap_r, cmap_c), p, final=False)

    lam_init = 0.8 - 0.6 * math.exp(-0.3 * 1)
    wqkv = attn_w_qkv[0].astype(BF16)
    q_l, k_l, v_l = _qkv(x, mod[1], norm1_g[1], wqkv, cmap, 0, n_lat, rope_l=l_lat)
    q_c, k_c, v_c, kf, vf = _qkv(x, mod[1], norm1_g[1], wqkv, cmap, n_lat, n_ctx)
    new_k = kf.reshape(b_ctx, 1, l_ctx, nh, 2, HEAD_DIM)
    new_v = vf.reshape(b_ctx, 1, l_ctx, nh, V_DIM)
    o_c = _attention(q_c.reshape(b_ctx, l_ctx, d), k_c.reshape(b_ctx, l_ctx, d), v_c.reshape(b_ctx, l_ctx, d),
                     attn_lam_q[0], attn_lam_k[0], attn_subln_g[0], lam_init)
    past = cache_k.shape[2]
    k_all = jnp.concatenate([k_l.reshape(b_lat, l_lat, d), cache_k[:, 0].reshape(b_lat, past, d).astype(BF16)], axis=1)
    v_all = jnp.concatenate([v_l.reshape(b_lat, l_lat, d), cache_v[:, 0].reshape(b_lat, past, d).astype(BF16)], axis=1)
    o_l = _attention(q_l.reshape(b_lat, l_lat, d), k_all, v_all,
                     attn_lam_q[0], attn_lam_k[0], attn_subln_g[0], lam_init)
    o = jnp.concatenate([o_l.reshape(n_lat, d), o_c.reshape(n_ctx, d)], axis=0)
    x = _oproj(x, o, mod[1], attn_w_o[0].astype(BF16), cmap)
    y = _moe(x, mod[1], 1, (cmap_r, cmap_c), p, final=True)

    return (y[n_lat:].reshape(b_ctx, l_ctx, d), y[:n_lat].reshape(b_lat, l_lat, d),
            new_state, new_k, new_v)
```

```python
import functools
import math

import jax
import jax.numpy as jnp
from jax import lax
from jax.experimental import pallas as pl
from jax.experimental.pallas import tpu as pltpu

F32 = jnp.float32
BF16 = jnp.bfloat16
I32 = jnp.int32
U32 = jnp.uint32

EPS = 1e-6
SSM_GROUP = 16
CHUNK = 16
GRID_W = 64
HEAD_DIM = 64
V_DIM = 2 * HEAD_DIM
ROPE_THETA = 10000.0
TOP_K = 8
N_EXPERT_GROUPS = 8
TOPK_GROUPS = 4
ROUTED_SCALE = 2.5
N_MOD = 6

LANES = 128
VMEM_LIMIT = 48 * 1024 * 1024

TM_DENSE = 512
TM_ROUTER = 256
TM_EXPERT = 512
TM_ROWS = 256
TQ_ATTN = 256
DMA_PRIORITIES = 2


def _params(sem):
    return pltpu.CompilerParams(dimension_semantics=sem, vmem_limit_bytes=VMEM_LIMIT)


def _normmod(x, g, sc, sh):
    ms = jnp.mean(x * x, axis=-1, keepdims=True)
    return (x * lax.rsqrt(ms + EPS) * g) * (1.0 + sc) + sh


def _cond_map(tm, n_lat, dec_seq):
    t_lat = n_lat // tm
    per = dec_seq // tm

    def f(i):
        return jnp.where(i < t_lat, 1 + i // per, 0)
    return f


def _ada_kernel(c_ref, w_ref, b_ref, o_ref):
    c = c_ref[...]
    s = (c * jax.nn.sigmoid(c)).astype(BF16)
    o_ref[0] = jnp.dot(s, w_ref[0].astype(BF16), preferred_element_type=F32) + b_ref[0]


def _ada(cond, ada_w, ada_b):
    depth, d, n6 = ada_w.shape
    nc = cond.shape[0]
    tn = 1536
    out = pl.pallas_call(
        _ada_kernel,
        grid=(depth, n6 // tn),
        in_specs=[pl.BlockSpec((nc, d), lambda l, j: (0, 0)),
                  pl.BlockSpec((1, d, tn), lambda l, j: (l, 0, j)),
                  pl.BlockSpec((1, 1, tn), lambda l, j: (l, 0, j))],
        out_specs=pl.BlockSpec((1, nc, tn), lambda l, j: (l, 0, j)),
        out_shape=jax.ShapeDtypeStruct((depth, nc, n6), F32),
        compiler_params=_params(("arbitrary", "arbitrary")),
        name="ada",
    )(cond, ada_w, ada_b.reshape(depth, 1, n6))
    return out.reshape(depth, nc, N_MOD, d)


GROUPS_PER_SLAB = LANES // SSM_GROUP
HALF_CHUNK = CHUNK // 2


def _chunk_perm():
    r = jnp.arange(HALF_CHUNK * LANES)
    t8, g8, c = r // LANES, (r // SSM_GROUP) % GROUPS_PER_SLAB, r % SSM_GROUP
    col = g8 * LANES + t8 * SSM_GROUP + c
    return (col[:, None] == r[None, :]).astype(BF16)


def _pack_kernel(x_ref, mod_ref, g_ref, p_ref, o_ref, h_s):
    m = mod_ref[0]
    h = _normmod(x_ref[...], g_ref[...], m[1:2], m[0:1])
    tm, d = h.shape
    nct = tm // CHUNK
    for j in range(d // LANES):
        h_s[j] = h[:, j * LANES:(j + 1) * LANES]
    slabs = []
    for j in range(d // LANES):
        for hf in range(2):
            slabs.append(jnp.concatenate(
                [h_s[j, pl.ds(hf * HALF_CHUNK + t8, nct, stride=CHUNK), :] for t8 in range(HALF_CHUNK)], axis=1))
    lhs = jnp.concatenate(slabs, axis=0).astype(BF16)
    out = jnp.dot(lhs, p_ref[...], preferred_element_type=F32)
    kw = CHUNK * SSM_GROUP
    for j in range(d // LANES):
        for hf in range(2):
            blk = out[(2 * j + hf) * nct:(2 * j + hf + 1) * nct]
            for g8 in range(GROUPS_PER_SLAB):
                col = (j * GROUPS_PER_SLAB + g8) * kw + hf * LANES
                o_ref[:, col:col + LANES] = blk[:, g8 * LANES:(g8 + 1) * LANES]


def _s5_pack(x, mod, g, cmap):
    n, d = x.shape
    tm = TM_DENSE
    return pl.pallas_call(
        _pack_kernel,
        grid=(n // tm,),
        in_specs=[pl.BlockSpec((tm, d), lambda i: (i, 0)),
                  pl.BlockSpec((1, N_MOD, d), lambda i: (cmap(i), 0, 0)),
                  pl.BlockSpec((1, d), lambda i: (0, 0)),
                  pl.BlockSpec((HALF_CHUNK * LANES, HALF_CHUNK * LANES), lambda i: (0, 0))],
        out_specs=pl.BlockSpec((tm // CHUNK, CHUNK * d), lambda i: (i, 0)),
        out_shape=jax.ShapeDtypeStruct((n // CHUNK, CHUNK * d), F32),
        scratch_shapes=[pltpu.VMEM((d // LANES, tm, LANES), F32)],
        compiler_params=_params(("arbitrary",)),
        name="s5_pack",
    )(x, mod, g.reshape(1, d), _chunk_perm())


def _s5_weights(lam_re, lam_im, log_dt, b_re, b_im, c_re, c_im):
    hi = lax.Precision.HIGHEST
    t = CHUNK
    w_in, w_out, toep, a16 = [], [], [], []
    for dr in range(2):
        dt = jnp.exp(log_dt[dr])[:, None]
        lr, li = lam_re[dr], lam_im[dr]
        mag = jnp.exp(lr * dt)
        a_re = mag * jnp.cos(li * dt)
        a_im = mag * jnp.sin(li * dt)
        den = lr * lr + li * li
        f_re = ((a_re - 1.0) * lr + a_im * li) / den
        f_im = (a_im * lr - (a_re - 1.0) * li) / den
        bb_re = f_re[..., None] * b_re[dr] - f_im[..., None] * b_im[dr]
        bb_im = f_re[..., None] * b_im[dr] + f_im[..., None] * b_re[dr]
        k = jnp.arange(t + 1, dtype=F32)[:, None, None]
        pm = jnp.exp(k * (lr * dt)[None])
        pk_re = pm * jnp.cos(k * (li * dt)[None])
        pk_im = pm * jnp.sin(k * (li * dt)[None])
        ks = jnp.arange(t)
        sel = (t - 1 - ks) if dr == 0 else ks
        pr, pi = pk_re[sel], pk_im[sel]
        wi_re = pr[..., None] * bb_re[None] - pi[..., None] * bb_im[None]
        wi_im = pr[..., None] * bb_im[None] + pi[..., None] * bb_re[None]
        g_, p_, c_ = bb_re.shape
        w_in.append((wi_re.transpose(1, 0, 3, 2).reshape(g_, t * c_, p_),
                     wi_im.transpose(1, 0, 3, 2).reshape(g_, t * c_, p_)))
        sel = (ks + 1) if dr == 0 else (t - ks)
        pr, pi = pk_re[sel], pk_im[sel]
        ca_re = c_re[dr][None] * pr[:, :, None, :] - c_im[dr][None] * pi[:, :, None, :]
        ca_im = c_re[dr][None] * pi[:, :, None, :] + c_im[dr][None] * pr[:, :, None, :]
        w_out.append((ca_re.transpose(1, 3, 0, 2).reshape(g_, p_, t * c_),
                      (-ca_im).transpose(1, 3, 0, 2).reshape(g_, p_, t * c_)))
        pr, pi = pk_re[:t], pk_im[:t]
        cd_re = c_re[dr][None] * pr[:, :, None, :] - c_im[dr][None] * pi[:, :, None, :]
        cd_im = c_re[dr][None] * pi[:, :, None, :] + c_im[dr][None] * pr[:, :, None, :]
        kk = jnp.einsum('kgop,gpi->kgoi', jnp.concatenate([cd_re, -cd_im], axis=3),
                        jnp.concatenate([bb_re, bb_im], axis=1), precision=hi)
        s_i = ks[:, None]
        t_i = ks[None, :]
        delta = (t_i - s_i) if dr == 0 else (s_i - t_i)
        kd = jnp.where((delta >= 0)[:, :, None, None, None], kk[jnp.clip(delta, 0, t - 1)], 0.0)
        toep.append(kd.transpose(2, 0, 4, 1, 3).reshape(g_, t * c_, t * c_))
        a16.append((pk_re[t], pk_im[t]))
    w_in_all = jnp.concatenate([w_in[0][0], w_in[1][0], w_in[0][1], w_in[1][1]], axis=2)
    w_out_all = jnp.concatenate([w_out[0][0], w_out[1][0], w_out[0][1], w_out[1][1]], axis=1)
    toep_all = toep[0] + toep[1]
    a16_all = jnp.stack([jnp.concatenate([a16[0][0], a16[1][0]], axis=1),
                         jnp.concatenate([a16[0][1], a16[1][1]], axis=1)], axis=1)
    return w_in_all.astype(BF16), toep_all.astype(BF16), w_out_all.astype(BF16), a16_all


def _s5_kernel(u_hbm, win_ref, wtp_ref, wout_ref, a16_ref, h0_ref, y_hbm, fin_ref,
               ubuf, ybuf, v_s, sf_s, sb_s, sem_in, sem_out, *, nc, nb, row0):
    g = pl.program_id(0)
    ng = pl.num_programs(0)
    kw = ubuf.shape[3]
    slot = g % 2

    def in_copy(grp, s, b):
        return pltpu.make_async_copy(u_hbm.at[pl.ds(row0 + b * nc, nc), pl.ds(grp * kw, kw)],
                                     ubuf.at[s, :, b, :], sem_in.at[s])

    def out_copy(grp, s, b):
        return pltpu.make_async_copy(ybuf.at[s, :, b, :],
                                     y_hbm.at[pl.ds(b * nc, nc), pl.ds(grp * kw, kw)], sem_out.at[s])

    @pl.when(g == 0)
    def _():
        for b in range(nb):
            in_copy(g, slot, b).start()

    @pl.when(g + 1 < ng)
    def _():
        for b in range(nb):
            in_copy(g + 1, 1 - slot, b).start()

    for b in range(nb):
        in_copy(g, slot, b).wait()
    u = ubuf[slot].reshape(nc * nb, kw).astype(BF16)
    v_s[...] = jnp.dot(u, win_ref[0], preferred_element_type=F32)
    a = a16_ref[0]
    ar, ai = a[0:1], a[1:2]
    fwd = (lax.broadcasted_iota(I32, (nb, 2 * LANES), 1) % LANES) < (LANES // 2)

    def step(i, carry):
        sr, si = carry
        st = jnp.concatenate([sr, si], axis=1)
        rf = pl.multiple_of(i * nb, nb)
        rb = pl.multiple_of((nc - 1 - i) * nb, nb)
        sf_s[pl.ds(rf, nb), :] = st
        sb_s[pl.ds(rb, nb), :] = st
        v = jnp.where(fwd, v_s[pl.ds(rf, nb), :], v_s[pl.ds(rb, nb), :])
        return ar * sr - ai * si + v[:, :LANES], ar * si + ai * sr + v[:, LANES:]

    h0 = h0_ref[0]
    sr, si = lax.fori_loop(0, nc, step, (h0[:, :LANES], h0[:, LANES:]))
    fin_ref[0] = jnp.concatenate([sr, si], axis=1)
    fwd_all = (lax.broadcasted_iota(I32, (nc * nb, 2 * LANES), 1) % LANES) < (LANES // 2)
    hs = jnp.where(fwd_all, sf_s[...], sb_s[...]).astype(BF16)
    y = (jnp.dot(u, wtp_ref[0], preferred_element_type=F32)
         + jnp.dot(hs, wout_ref[0], preferred_element_type=F32))

    @pl.when(g >= 2)
    def _():
        for b in range(nb):
            out_copy(g - 2, slot, b).wait()

    ybuf[slot] = y.reshape(nc, nb, kw)
    for b in range(nb):
        out_copy(g, slot, b).start()

    @pl.when(g == ng - 1)
    def _():
        if ng > 1:
            for b in range(nb):
                out_copy(g - 1, 1 - slot, b).wait()
        for b in range(nb):
            out_copy(g, slot, b).wait()


def _s5(u, h0, wts, row0, nb, nc):
    w_in, w_tp, w_out, a16 = wts
    r = nb * nc
    kw = CHUNK * SSM_GROUP
    g = u.shape[1] // kw
    return pl.pallas_call(
        functools.partial(_s5_kernel, nc=nc, nb=nb, row0=row0),
        grid=(g,),
        in_specs=[pl.BlockSpec(memory_space=pl.ANY),
                  pl.BlockSpec((1, kw, kw), lambda j: (j, 0, 0)),
                  pl.BlockSpec((1, kw, kw), lambda j: (j, 0, 0)),
                  pl.BlockSpec((1, kw, kw), lambda j: (j, 0, 0)),
                  pl.BlockSpec((1, 2, LANES), lambda j: (j, 0, 0)),
                  pl.BlockSpec((1, nb, kw), lambda j: (j, 0, 0))],
        out_specs=[pl.BlockSpec(memory_space=pl.ANY),
                   pl.BlockSpec((1, nb, kw), lambda j: (j, 0, 0))],
        out_shape=[jax.ShapeDtypeStruct((r, u.shape[1]), F32),
                   jax.ShapeDtypeStruct((g, nb, kw), F32)],
        scratch_shapes=[pltpu.VMEM((2, nc, nb, kw), F32), pltpu.VMEM((2, nc, nb, kw), F32),
                        pltpu.VMEM((r, kw), F32), pltpu.VMEM((r, kw), F32), pltpu.VMEM((r, kw), F32),
                        pltpu.SemaphoreType.DMA((2,)), pltpu.SemaphoreType.DMA((2,))],
        compiler_params=_params(("arbitrary",)),
        name="s5_scan",
    )(u, w_in, w_tp, w_out, a16, h0)


def _glu_kernel(x_ref, ya_ref, yb_ref, mod_ref, g_ref, d_ref, pt_ref, w_ref, o_ref, y_s, *, tiles_a):
    tm, dd = x_ref.shape
    nct = tm // CHUNK
    kw = CHUNK * SSM_GROUP
    from_a = pl.program_id(0) < tiles_a
    slabs = []
    for j in range(dd // LANES):
        for hf in range(2):
            cols = [(j * GROUPS_PER_SLAB + g8) * kw + hf * LANES for g8 in range(GROUPS_PER_SLAB)]
            slabs.append(jnp.concatenate(
                [jnp.where(from_a, ya_ref[:, c0:c0 + LANES], yb_ref[:, c0:c0 + LANES]) for c0 in cols], axis=1))
    lhs = jnp.concatenate(slabs, axis=0)
    hi = lhs.astype(BF16)
    lo = (lhs - hi.astype(F32)).astype(BF16)
    pt = pt_ref[...]
    nat = jnp.dot(hi, pt, preferred_element_type=F32) + jnp.dot(lo, pt, preferred_element_type=F32)
    for j in range(dd // LANES):
        for hf in range(2):
            blk = nat[(2 * j + hf) * nct:(2 * j + hf + 1) * nct]
            for t8 in range(HALF_CHUNK):
                y_s[j, pl.ds(hf * HALF_CHUNK + t8, nct, stride=CHUNK), :] = blk[:, t8 * LANES:(t8 + 1) * LANES]
    m = mod_ref[0]
    x = x_ref[...]
    h = _normmod(x, g_ref[...], m[1:2], m[0:1])
    y = d_ref[...] * h + jnp.concatenate([y_s[j] for j in range(dd // LANES)], axis=1)
    gl = jax.nn.gelu(y).astype(BF16)
    z = jnp.dot(gl, w_ref[...], preferred_element_type=F32)
    o_ref[...] = x + m[2:3] * (z[:, :dd] * jax.nn.sigmoid(z[:, dd:]))


def _glu(x, y_a, y_b, mod, g, dskip, w, cmap):
    n, d = x.shape
    tm = TM_DENSE
    nct = tm // CHUNK
    tiles_a = y_a.shape[0] // nct
    last_a = tiles_a - 1
    return pl.pallas_call(
        functools.partial(_glu_kernel, tiles_a=tiles_a),
        grid=(n // tm,),
        in_specs=[pl.BlockSpec((tm, d), lambda i: (i, 0)),
                  pl.BlockSpec((nct, CHUNK * d), lambda i: (jnp.minimum(i, last_a), 0)),
                  pl.BlockSpec((nct, CHUNK * d), lambda i: (jnp.maximum(i - tiles_a, 0), 0)),
                  pl.BlockSpec((1, N_MOD, d), lambda i: (cmap(i), 0, 0)),
                  pl.BlockSpec((1, d), lambda i: (0, 0)),
                  pl.BlockSpec((1, d), lambda i: (0, 0)),
                  pl.BlockSpec((HALF_CHUNK * LANES, HALF_CHUNK * LANES), lambda i: (0, 0)),
                  pl.BlockSpec((d, 2 * d), lambda i: (0, 0))],
        out_specs=pl.BlockSpec((tm, d), lambda i: (i, 0)),
        out_shape=jax.ShapeDtypeStruct((n, d), F32),
        scratch_shapes=[pltpu.VMEM((d // LANES, tm, LANES), F32)],
        compiler_params=_params(("arbitrary",)),
        name="s5_glu",
    )(x, y_a, y_b, mod, g.reshape(1, d), dskip.reshape(1, d), _chunk_perm().T, w)


def _rope_tables(l):
    r = HEAD_DIM // 2
    half = r // 2
    freq = ROPE_THETA ** (-jnp.arange(half, dtype=F32) / half)
    pos = jnp.arange(l)
    row = (pos // GRID_W).astype(F32)
    col = (pos % GRID_W).astype(F32)
    d = jnp.arange(LANES) % HEAD_DIM
    p = jnp.where((d < r)[None, :], row[:, None], col[:, None])
    ang = p * freq[d % half][None, :]
    sign = jnp.where((d % r) < half, -1.0, 1.0)[None, :]
    return jnp.cos(ang), jnp.sin(ang) * sign


def _qkv_kernel(*refs, rope):
    if rope:
        x_ref, mod_ref, g_ref, w_ref, cos_ref, sin_ref, q_ref, k_ref, v_ref = refs
    else:
        x_ref, mod_ref, g_ref, w_ref, q_ref, k_ref, v_ref, kf_ref, vf_ref = refs
    m = mod_ref[0]
    h = _normmod(x_ref[...], g_ref[...], m[1:2], m[0:1]).astype(BF16)
    qkv = jnp.dot(h, w_ref[...], preferred_element_type=F32)
    d = x_ref.shape[1]
    q = qkv[:, :d] * (HEAD_DIM ** -0.5 * math.log2(math.e))
    k = qkv[:, d:2 * d]
    v = qkv[:, 2 * d:]
    v_ref[...] = v.astype(BF16)
    if not rope:
        q_ref[...] = q.astype(BF16)
        k_ref[...] = k.astype(BF16)
        kf_ref[...] = k
        vf_ref[...] = v
        return
    cos = cos_ref[...]
    sin = sin_ref[...]
    half = HEAD_DIM // 4
    lane = lax.broadcasted_iota(I32, cos.shape, 1)
    up_sel = (lane % (2 * half)) < half
    for src, dst in ((q, q_ref), (k, k_ref)):
        for j in range(d // LANES):
            blk = src[:, j * LANES:(j + 1) * LANES]
            up = pltpu.roll(blk, LANES - half, 1)
            dn = pltpu.roll(blk, half, 1)
            dst[:, j * LANES:(j + 1) * LANES] = (blk * cos + jnp.where(up_sel, up, dn) * sin).astype(BF16)


def _qkv(x, mod, g, w, cmap, row0, nrows, rope_l=None):
    n, d = x.shape
    tm = TM_DENSE
    t0 = row0 // tm
    rope = rope_l is not None
    in_specs = [pl.BlockSpec((tm, d), lambda i: (i + t0, 0)),
                pl.BlockSpec((1, N_MOD, d), lambda i: (cmap(i + t0), 0, 0)),
                pl.BlockSpec((1, d), lambda i: (0, 0)),
                pl.BlockSpec((d, 3 * d), lambda i: (0, 0))]
    args = [x, mod, g.reshape(1, d), w]
    row_spec = pl.BlockSpec((tm, d), lambda i: (i, 0))
    out_specs = [row_spec] * 3
    out_shape = [jax.ShapeDtypeStruct((nrows, d), BF16)] * 3
    if rope:
        per = rope_l // tm
        cos, sin = _rope_tables(rope_l)
        in_specs += [pl.BlockSpec((tm, LANES), lambda i: (i % per, 0))] * 2
        args += [cos, sin]
    else:
        out_specs += [row_spec] * 2
        out_shape += [jax.ShapeDtypeStruct((nrows, d), F32)] * 2
    return pl.pallas_call(
        functools.partial(_qkv_kernel, rope=rope),
        grid=(nrows // tm,),
        in_specs=in_specs,
        out_specs=out_specs,
        out_shape=out_shape,
        compiler_params=_params(("arbitrary",)),
        name="qkv_rope" if rope else "qkv",
    )(*args)


def _attn_kernel(q_ref, kt_ref, v_ref, lq_ref, lk_ref, g_ref, o_ref, *, lam_init):
    q = q_ref[0]
    tq = q.shape[0]
    lane = lax.broadcasted_iota(I32, q.shape, 1)
    zero = jnp.zeros_like(q)
    qq = jnp.concatenate([jnp.where(lane < HEAD_DIM, q, zero), jnp.where(lane >= HEAD_DIM, q, zero)], axis=0)
    s = jnp.dot(qq, kt_ref[0, 0], preferred_element_type=F32)
    p = jnp.exp2(s - jnp.max(s, axis=1, keepdims=True)).astype(BF16)
    pv = jnp.dot(p, v_ref[0], preferred_element_type=F32)
    l = pv[:, V_DIM:V_DIM + 1]
    lqk = lq_ref[...] * lk_ref[...]
    lam = (jnp.exp(jnp.sum(lqk[0:1], axis=1, keepdims=True))
           - jnp.exp(jnp.sum(lqk[1:2], axis=1, keepdims=True)) + lam_init)
    o = pv[:tq, :V_DIM] * (1.0 / l[:tq]) - pv[tq:, :V_DIM] * (lam / l[tq:])
    o = o * lax.rsqrt(jnp.mean(o * o, axis=1, keepdims=True) + EPS) * g_ref[...]
    o_ref[0] = (o * (1.0 - lam_init)).astype(o_ref.dtype)


def _attention(q, k, v, lam_q, lam_k, subln_g, lam_init):
    nb, l, d = q.shape
    lk = k.shape[1]
    nh = d // V_DIM
    kt = k.reshape(nb, lk, nh, V_DIM).transpose(0, 2, 3, 1)
    v1 = jnp.concatenate([v.reshape(nb, lk, nh, V_DIM), jnp.ones((nb, lk, nh, V_DIM), BF16)],
                         axis=3).reshape(nb, lk, 2 * d)
    tq = min(TQ_ATTN, l)
    return pl.pallas_call(
        functools.partial(_attn_kernel, lam_init=lam_init),
        grid=(nb, nh, l // tq),
        in_specs=[pl.BlockSpec((1, tq, V_DIM), lambda b, h, i: (b, i, h)),
                  pl.BlockSpec((1, 1, V_DIM, lk), lambda b, h, i: (b, h, 0, 0)),
                  pl.BlockSpec((1, lk, 2 * V_DIM), lambda b, h, i: (b, 0, h)),
                  pl.BlockSpec((2, HEAD_DIM), lambda b, h, i: (0, 0)),
                  pl.BlockSpec((2, HEAD_DIM), lambda b, h, i: (0, 0)),
                  pl.BlockSpec((1, V_DIM), lambda b, h, i: (0, 0))],
        out_specs=pl.BlockSpec((1, tq, V_DIM), lambda b, h, i: (b, i, h)),
        out_shape=jax.ShapeDtypeStruct((nb, l, d), BF16),
        compiler_params=_params(("arbitrary", "arbitrary", "arbitrary")),
        name="diff_attn",
    )(q, kt, v1, lam_q, lam_k, subln_g.reshape(1, V_DIM))


def _oproj_kernel(x_ref, o_ref, mod_ref, w_ref, y_ref):
    m = mod_ref[0]
    y_ref[...] = x_ref[...] + m[2:3] * jnp.dot(o_ref[...], w_ref[...], preferred_element_type=F32)


def _oproj(x, o, mod, w, cmap):
    n, d = x.shape
    tm = TM_DENSE
    return pl.pallas_call(
        _oproj_kernel,
        grid=(n // tm,),
        in_specs=[pl.BlockSpec((tm, d), lambda i: (i, 0)),
                  pl.BlockSpec((tm, d), lambda i: (i, 0)),
                  pl.BlockSpec((1, N_MOD, d), lambda i: (cmap(i), 0, 0)),
                  pl.BlockSpec((d, d), lambda i: (0, 0))],
        out_specs=pl.BlockSpec((tm, d), lambda i: (i, 0)),
        out_shape=jax.ShapeDtypeStruct((n, d), F32),
        compiler_params=_params(("arbitrary",)),
        name="attn_out",
    )(x, o, mod, w)


def _pack_rows(h):
    half = h.shape[1] // 2
    bits = lax.bitcast_convert_type(h.astype(BF16).astype(F32), U32)
    return bits[:, half:] | (bits[:, :half] >> 16)


def _unpack_rows(w):
    lo = lax.bitcast_convert_type(w << 16, F32)
    hi = lax.bitcast_convert_type(w & jnp.uint32(0xFFFF0000), F32)
    return lo, hi


def _stack_rows(rows, width):
    ri = lax.broadcasted_iota(I32, (len(rows), width), 0)
    out = jnp.zeros((len(rows), width), rows[0].dtype)
    for j, r in enumerate(rows):
        out = jnp.where(ri == j, r, out)
    return out


def _router_kernel(x_ref, mod_ref, g_ref, wh_ref, wl_ref, b_ref, h_ref, idx_ref, w_ref, pos_ref, cnt_ref, cnt_s):
    step = pl.program_id(0)

    @pl.when(step == 0)
    def _():
        cnt_s[...] = jnp.zeros_like(cnt_s)

    m = mod_ref[0]
    h = _normmod(x_ref[...], g_ref[...], m[4:5], m[3:4])
    h_ref[...] = _pack_rows(h)
    tr = h.shape[0]
    ne = wh_ref.shape[0]
    per = ne // N_EXPERT_GROUPS
    hh = h.astype(BF16)
    hl = (h - hh.astype(F32)).astype(BF16)
    nt = (((1,), (1,)), ((), ()))
    wh = wh_ref[...]
    logits = (lax.dot_general(wh, hh, nt, preferred_element_type=F32)
              + lax.dot_general(wl_ref[...], hh, nt, preferred_element_type=F32)
              + lax.dot_general(wh, hl, nt, preferred_element_type=F32))
    scores = jax.nn.sigmoid(logits)
    biased = scores + b_ref[...]
    neg = -jnp.inf
    ri = lax.broadcasted_iota(I32, (per, tr), 0).astype(F32)
    blocks, gscores = [], []
    for j in range(N_EXPERT_GROUPS):
        blk = biased[j * per:(j + 1) * per, :]
        m1 = jnp.max(blk, axis=0, keepdims=True)
        f1 = jnp.min(jnp.where(blk == m1, ri, float(per)), axis=0, keepdims=True)
        m2 = jnp.max(jnp.where(ri == f1, neg, blk), axis=0, keepdims=True)
        blocks.append(blk)
        gscores.append(m1 + m2)
    gs = _stack_rows(gscores, tr)
    gi = lax.broadcasted_iota(I32, gs.shape, 0).astype(F32)
    gsel = jnp.zeros(gs.shape, F32)
    for _ in range(TOPK_GROUPS):
        mm = jnp.max(gs, axis=0, keepdims=True)
        pick = gi == jnp.min(jnp.where(gs == mm, gi, float(N_EXPERT_GROUPS)), axis=0, keepdims=True)
        gsel = jnp.where(pick, 1.0, gsel)
        gs = jnp.where(pick, neg, gs)
    masked = jnp.concatenate(
        [jnp.where(gsel[j:j + 1, :] > 0.0, blocks[j], neg) for j in range(N_EXPERT_GROUPS)], axis=0)
    ei = lax.broadcasted_iota(I32, masked.shape, 0).astype(F32)
    picks, idxs, ws = [], [], []
    for _ in range(TOP_K):
        mm = jnp.max(masked, axis=0, keepdims=True)
        ff = jnp.min(jnp.where(masked == mm, ei, float(ne)), axis=0, keepdims=True)
        pick = ei == ff
        picks.append(pick)
        idxs.append(ff)
        ws.append(jnp.sum(jnp.where(pick, scores, 0.0), axis=0, keepdims=True))
        masked = jnp.where(pick, neg, masked)
    wsum = ws[0]
    for wk in ws[1:]:
        wsum = wsum + wk
    w_ref[...] = _stack_rows([wk / wsum * ROUTED_SCALE for wk in ws], tr)
    idx_ref[...] = _stack_rows(idxs, tr).astype(I32)
    sel = jnp.zeros(masked.shape, F32)
    for pick in picks:
        sel = sel + jnp.where(pick, 1.0, 0.0)
    earlier = (lax.broadcasted_iota(I32, (tr, tr), 0) < lax.broadcasted_iota(I32, (tr, tr), 1))
    prefix = jnp.dot(sel.astype(BF16), jnp.where(earlier, 1.0, 0.0).astype(BF16), preferred_element_type=F32)
    base = prefix + cnt_s[...]
    pos_ref[...] = _stack_rows(
        [jnp.sum(jnp.where(pick, base, 0.0), axis=0, keepdims=True) for pick in picks], tr).astype(I32)
    cnt_s[...] = cnt_s[...] + jnp.sum(sel, axis=1, keepdims=True)
    cnt_ref[...] = cnt_s[...]


def _router(x, mod, g, w_t_hi, w_t_lo, bias, cmap):
    n, d = x.shape
    ne = w_t_hi.shape[0]
    tr = TM_ROUTER
    tok_spec = pl.BlockSpec((TOP_K, tr), lambda i: (0, i))
    return pl.pallas_call(
        _router_kernel,
        grid=(n // tr,),
        in_specs=[pl.BlockSpec((tr, d), lambda i: (i, 0)),
                  pl.BlockSpec((1, N_MOD, d), lambda i: (cmap(i), 0, 0)),
                  pl.BlockSpec((1, d), lambda i: (0, 0)),
                  pl.BlockSpec((ne, d), lambda i: (0, 0)),
                  pl.BlockSpec((ne, d), lambda i: (0, 0)),
                  pl.BlockSpec((ne, 1), lambda i: (0, 0))],
        out_specs=[pl.BlockSpec((tr, d // 2), lambda i: (i, 0)), tok_spec, tok_spec, tok_spec,
                   pl.BlockSpec((ne, 1), lambda i: (0, 0))],
        out_shape=[jax.ShapeDtypeStruct((n, d // 2), U32),
                   jax.ShapeDtypeStruct((TOP_K, n), I32),
                   jax.ShapeDtypeStruct((TOP_K, n), F32),
                   jax.ShapeDtypeStruct((TOP_K, n), I32),
                   jax.ShapeDtypeStruct((ne, 1), F32)],
        scratch_shapes=[pltpu.VMEM((ne, 1), F32)],
        compiler_params=_params(("arbitrary",)),
        name="moe_router",
    )(x, mod, g.reshape(1, d), w_t_hi, w_t_lo, bias.reshape(ne, 1))


def _dest_kernel(idx_ref, pos_ref, start_ref, o_ref):
    idx = idx_ref[...]
    ei = lax.broadcasted_iota(I32, (start_ref.shape[0], idx.shape[1]), 0)
    start = start_ref[...]
    rows = [jnp.sum(jnp.where(ei == idx[k:k + 1], start, 0.0), axis=0, keepdims=True) for k in range(TOP_K)]
    o_ref[...] = _stack_rows(rows, idx.shape[1]).astype(I32) + pos_ref[...]


def _dest(idx, pos, start):
    n = idx.shape[1]
    ne = start.shape[0]
    tr = 512
    tok_spec = pl.BlockSpec((TOP_K, tr), lambda i: (0, i))
    return pl.pallas_call(
        _dest_kernel,
        grid=(n // tr,),
        in_specs=[tok_spec, tok_spec, pl.BlockSpec((ne, 1), lambda i: (0, 0))],
        out_specs=tok_spec,
        out_shape=jax.ShapeDtypeStruct((TOP_K, n), I32),
        compiler_params=_params(("arbitrary",)),
        name="moe_dest",
    )(idx, pos, start)


def _dispatch_kernel(dest_ref, h_ref, xs_ref, sem):
    tm = h_ref.shape[0]

    def issue(t, carry):
        for k in range(TOP_K):
            pltpu.make_async_copy(h_ref.at[pl.ds(t, 1), :], xs_ref.at[pl.ds(dest_ref[k, t], 1), :],
                                  sem).start(priority=k % DMA_PRIORITIES)
        return carry

    lax.fori_loop(0, tm, issue, 0)
    for _ in range(TOP_K):
        pltpu.make_async_copy(h_ref, xs_ref.at[pl.ds(0, tm), :], sem).wait()


def _dispatch(h, dest):
    n, d = h.shape
    tm = TM_ROWS
    return pl.pallas_call(
        _dispatch_kernel,
        grid=(n // tm,),
        in_specs=[pl.BlockSpec((TOP_K, tm), lambda i: (0, i), memory_space=pltpu.SMEM),
                  pl.BlockSpec((tm, d), lambda i: (i, 0))],
        out_specs=pl.BlockSpec(memory_space=pl.ANY),
        out_shape=jax.ShapeDtypeStruct((n * TOP_K, d), U32),
        scratch_shapes=[pltpu.SemaphoreType.DMA(())],
        compiler_params=_params(("arbitrary",)),
        name="moe_dispatch",
    )(dest, h)


def _expert_kernel(tile_ref, exp_ref, lo_ref, hi_ref, newe_ref, newt_ref, last_ref,
                   xs_ref, w1_ref, w3_ref, w2_ref, ys_ref, w1_s, w3_s, w2_s, acc_s):
    i = pl.program_id(0)

    @pl.when(newe_ref[i] == 1)
    def _():
        w1_s[...] = w1_ref[0, 0].astype(BF16)
        w3_s[...] = w3_ref[0, 0].astype(BF16)
        w2_s[...] = w2_ref[0, 0].astype(BF16)

    tm = xs_ref.shape[0]
    lo, hi = lo_ref[i], hi_ref[i]

    @pl.when(hi > lo)
    def _():
        x_lo, x_hi = _unpack_rows(xs_ref[...])
        x = jnp.concatenate([x_lo, x_hi], axis=1).astype(BF16)
        a = jnp.dot(x, w1_s[...], preferred_element_type=F32)
        b = jnp.dot(x, w3_s[...], preferred_element_type=F32)
        row = tile_ref[i] * tm + lax.broadcasted_iota(I32, (tm, 1), 0)
        hb = jnp.where((row >= lo) & (row < hi), a * jax.nn.sigmoid(a) * b, 0.0).astype(BF16)
        y = jnp.dot(hb, w2_s[...], preferred_element_type=F32)
        first, last = newt_ref[i] == 1, last_ref[i] == 1

        @pl.when(first & last)
        def _():
            ys_ref[...] = _pack_rows(y)

        @pl.when(first & jnp.logical_not(last))
        def _():
            acc_s[...] = y

        @pl.when(jnp.logical_not(first))
        def _():
            acc_s[...] = acc_s[...] + y

    @pl.when((last_ref[i] == 1) & jnp.logical_not((newt_ref[i] == 1) & (hi > lo)))
    def _():
        ys_ref[...] = _pack_rows(acc_s[...])


def _work_items(counts, n_rows, tm):
    ne = counts.shape[0]
    n_tiles = n_rows // tm
    n_items = n_tiles + ne - 1
    end = jnp.cumsum(counts)
    start = end - counts
    first_t = start // tm
    n_e = jnp.where(counts > 0, (end - 1) // tm - first_t + 1, 0)
    item_end = jnp.cumsum(n_e)
    item_off = item_end - n_e
    it = jnp.arange(n_items, dtype=I32)
    total = item_end[-1]
    valid = it < total
    last_item = jnp.minimum(it, total - 1)
    e = jnp.minimum(jnp.sum((item_end[None, :] <= last_item[:, None]).astype(I32), axis=1), ne - 1).astype(I32)
    tile = jnp.where(valid, first_t[e] + it - item_off[e], n_tiles - 1).astype(I32)
    lo = jnp.where(valid, jnp.maximum(start[e], tile * tm), 0).astype(I32)
    hi = jnp.where(valid, jnp.minimum(end[e], (tile + 1) * tm), 0).astype(I32)
    prev_e = jnp.concatenate([jnp.full((1,), -1, I32), e[:-1]])
    prev_t = jnp.concatenate([jnp.full((1,), -1, I32), tile[:-1]])
    next_t = jnp.concatenate([tile[1:], jnp.full((1,), -1, I32)])
    return (tile, e, lo, hi, (e != prev_e).astype(I32), (tile != prev_t).astype(I32),
            (tile != next_t).astype(I32))


def _experts(xs, items, w1, w3, w2, layer):
    n_rows, dw = xs.shape
    d, de = w1.shape[2], w1.shape[3]
    tm = TM_EXPERT
    n_items = items[0].shape[0]
    grid_spec = pltpu.PrefetchScalarGridSpec(
        num_scalar_prefetch=len(items),
        grid=(n_items,),
        in_specs=[pl.BlockSpec((tm, dw), lambda i, t, e, *_: (t[i], 0)),
                  pl.BlockSpec((1, 1, d, de), lambda i, t, e, *_: (layer, e[i], 0, 0)),
                  pl.BlockSpec((1, 1, d, de), lambda i, t, e, *_: (layer, e[i], 0, 0)),
                  pl.BlockSpec((1, 1, de, d), lambda i, t, e, *_: (layer, e[i], 0, 0))],
        out_specs=pl.BlockSpec((tm, dw), lambda i, t, e, *_: (t[i], 0)),
        scratch_shapes=[pltpu.VMEM((d, de), BF16), pltpu.VMEM((d, de), BF16), pltpu.VMEM((de, d), BF16),
                        pltpu.VMEM((tm, d), F32)],
    )
    return pl.pallas_call(
        _expert_kernel,
        grid_spec=grid_spec,
        out_shape=jax.ShapeDtypeStruct((n_rows, dw), U32),
        compiler_params=_params(("arbitrary",)),
        name="moe_experts",
    )(*items, xs, w1, w3, w2)


def _combine_kernel(dest_ref, dest_next_ref, x_ref, h_ref, w_ref, mod_ref, s1_ref, s3_ref, s2_ref, fg_ref, ys_ref,
                    o_ref, buf, sem, *, final):
    i = pl.program_id(0)
    tm = x_ref.shape[0]
    slot = i % 2

    def issue(dref, s):
        def body(t, carry):
            for k in range(TOP_K):
                pltpu.make_async_copy(ys_ref.at[pl.ds(dref[k, t], 1), :], buf.at[s, k, pl.ds(t, 1), :],
                                      sem.at[s]).start(priority=k % DMA_PRIORITIES)
            return carry
        lax.fori_loop(0, tm, body, 0)

    @pl.when(i == 0)
    def _():
        issue(dest_ref, slot)

    @pl.when(i + 1 < pl.num_programs(0))
    def _():
        issue(dest_next_ref, 1 - slot)

    h_lo, h_hi = _unpack_rows(h_ref[...])
    hb = jnp.concatenate([h_lo, h_hi], axis=1).astype(BF16)
    a = jnp.dot(hb, s1_ref[...], preferred_element_type=F32)
    b = jnp.dot(hb, s3_ref[...], preferred_element_type=F32)
    acc = jnp.dot((a * jax.nn.sigmoid(a) * b).astype(BF16), s2_ref[...], preferred_element_type=F32)
    for k in range(TOP_K):
        pltpu.make_async_copy(ys_ref.at[pl.ds(0, tm), :], buf.at[slot, k], sem.at[slot]).wait()
    w = w_ref[...]
    half = acc.shape[1] // 2
    acc_lo, acc_hi = acc[:, :half], acc[:, half:]
    for k in range(TOP_K):
        y_lo, y_hi = _unpack_rows(buf[slot, k])
        acc_lo = acc_lo + w[:, k:k + 1] * y_lo
        acc_hi = acc_hi + w[:, k:k + 1] * y_hi
    acc = jnp.concatenate([acc_lo, acc_hi], axis=1)
    m = mod_ref[0]
    y = x_ref[...] + m[5:6] * acc
    if final:
        y = y * lax.rsqrt(jnp.mean(y * y, axis=-1, keepdims=True) + EPS) * fg_ref[...]
    o_ref[...] = y


def _combine(x, h, ys, dest, w, mod, s1, s3, s2, final_g, cmap, final):
    n, d = x.shape
    ds_ = s1.shape[1]
    tm = TM_ROWS
    last = n // tm - 1
    return pl.pallas_call(
        functools.partial(_combine_kernel, final=final),
        grid=(n // tm,),
        in_specs=[pl.BlockSpec((TOP_K, tm), lambda i: (0, i), memory_space=pltpu.SMEM),
                  pl.BlockSpec((TOP_K, tm), lambda i: (0, jnp.minimum(i + 1, last)), memory_space=pltpu.SMEM),
                  pl.BlockSpec((tm, d), lambda i: (i, 0)),
                  pl.BlockSpec((tm, d // 2), lambda i: (i, 0)),
                  pl.BlockSpec((tm, TOP_K), lambda i: (i, 0)),
                  pl.BlockSpec((1, N_MOD, d), lambda i: (cmap(i), 0, 0)),
                  pl.BlockSpec((d, ds_), lambda i: (0, 0)),
                  pl.BlockSpec((d, ds_), lambda i: (0, 0)),
                  pl.BlockSpec((ds_, d), lambda i: (0, 0)),
                  pl.BlockSpec((1, d), lambda i: (0, 0)),
                  pl.BlockSpec(memory_space=pl.ANY)],
        out_specs=pl.BlockSpec((tm, d), lambda i: (i, 0)),
        out_shape=jax.ShapeDtypeStruct((n, d), F32),
        scratch_shapes=[pltpu.VMEM((2, TOP_K, tm, d // 2), U32), pltpu.SemaphoreType.DMA((2,))],
        compiler_params=_params(("arbitrary",)),
        name="moe_combine",
    )(dest, dest, x, h, w, mod, s1, s3, s2, final_g.reshape(1, d), ys)


def _moe(x, mod, layer, cmaps, p, final):
    cmap_r, cmap_c = cmaps
    n, d = x.shape
    rw = p['moe_router_w'][layer].T
    rw_hi = rw.astype(BF16)
    rw_lo = (rw - rw_hi.astype(F32)).astype(BF16)
    h, idx, wts, pos, cnt = _router(x, mod, p['norm2_g'][layer], rw_hi, rw_lo, p['moe_router_b'][layer], cmap_r)
    counts = cnt[:, 0].astype(I32)
    start = jnp.cumsum(counts) - counts
    dest = _dest(idx, pos, start.astype(F32).reshape(-1, 1))
    xs = _dispatch(h, dest)
    items = _work_items(counts, n * TOP_K, TM_EXPERT)
    ys = _experts(xs, items, p['moe_w1'], p['moe_w3'], p['moe_w2'], layer)
    return _combine(x, h, ys, dest, wts.T, mod,
                    p['moe_shared_w1'][layer].astype(BF16), p['moe_shared_w3'][layer].astype(BF16),
                    p['moe_shared_w2'][layer].astype(BF16), p['final_g'], cmap_c, final)


def kernel(x_prompt, x_sample, c, state_ssm, cache_k, cache_v, c_ctx, norm1_g, norm2_g, ada_w, ada_b, ssm_lam_re, ssm_lam_im, ssm_log_dt, ssm_b_re, ssm_b_im, ssm_c_re, ssm_c_im, ssm_d, ssm_glu_w, attn_w_qkv, attn_lam_q, attn_lam_k, attn_subln_g, attn_w_o, moe_router_w, moe_router_b, moe_w1, moe_w3, moe_w2, moe_shared_w1, moe_shared_w3, moe_shared_w2, final_g):
    p = dict(norm2_g=norm2_g, moe_router_w=moe_router_w, moe_router_b=moe_router_b,
             moe_w1=moe_w1, moe_w3=moe_w3, moe_w2=moe_w2, moe_shared_w1=moe_shared_w1,
             moe_shared_w3=moe_shared_w3, moe_shared_w2=moe_shared_w2, final_g=final_g)
    b_ctx, l_ctx, d = x_prompt.shape
    b_lat, l_lat, _ = x_sample.shape
    n_ctx, n_lat = b_ctx * l_ctx, b_lat * l_lat
    g = d // SSM_GROUP
    pdim = ssm_lam_re.shape[-1]
    nh = d // V_DIM
    assert 2 * pdim == LANES and CHUNK * SSM_GROUP == 2 * LANES
    assert n_ctx % TM_DENSE == 0 and l_lat % TM_DENSE == 0 and b_lat % 8 == 0 and b_ctx % 8 == 0
    nc_lat, nc_ctx = l_lat // CHUNK, l_ctx // CHUNK
    assert (b_lat * nc_lat) % (b_ctx * nc_ctx) == 0
    cmap = _cond_map(TM_DENSE, n_lat, l_lat)
    cmap_r = _cond_map(TM_ROUTER, n_lat, l_lat)
    cmap_c = _cond_map(TM_ROWS, n_lat, l_lat)

    nc = 1 + b_lat
    nc_pad = -(-nc // 8) * 8
    cond = jnp.concatenate([c_ctx[None, :], c, jnp.zeros((nc_pad - nc, d), F32)], axis=0)
    mod = _ada(cond, ada_w, ada_b)

    x = jnp.concatenate([x_sample.reshape(n_lat, d), x_prompt.reshape(n_ctx, d)], axis=0)

    u = _s5_pack(x, mod[0], norm1_g[0], cmap)
    wts = _s5_weights(ssm_lam_re[0], ssm_lam_im[0], ssm_log_dt[0], ssm_b_re[0], ssm_b_im[0],
                      ssm_c_re[0], ssm_c_im[0])
    h0 = state_ssm[:, 0].transpose(3, 0, 2, 1, 4).reshape(g, b_lat, 2 * LANES)
    y_lat, _ = _s5(u, h0, wts, 0, b_lat, nc_lat)
    y_ctx, fin = _s5(u, jnp.zeros((g, b_ctx, 2 * LANES), F32), wts, b_lat * nc_lat, b_ctx, nc_ctx)
    new_state = fin.reshape(g, b_ctx, 2, 2, pdim).transpose(1, 3, 2, 0, 4)[:, None]
    x = _glu(x, y_lat, y_ctx, mod[0], norm1_g[0], ssm_d[0], ssm_glu_w[0].astype(BF16), cmap)
    x = _moe(x, mod[0], 0, (cmap_r, cmap_c), p, final=False)

    lam_init = 0.8 - 0.6 * math.exp(-0.3 * 1)
    wqkv = attn_w_qkv[0].astype(BF16)
    q_l, k_l, v_l = _qkv(x, mod[1], norm1_g[1], wqkv, cmap, 0, n_lat, rope_l=l_lat)
    q_c, k_c, v_c, kf, vf = _qkv(x, mod[1], norm1_g[1], wqkv, cmap, n_lat, n_ctx)
    new_k = kf.reshape(b_ctx, 1, l_ctx, nh, 2, HEAD_DIM)
    new_v = vf.reshape(b_ctx, 1, l_ctx, nh, V_DIM)
    o_c = _attention(q_c.reshape(b_ctx, l_ctx, d), k_c.reshape(b_ctx, l_ctx, d), v_c.reshape(b_ctx, l_ctx, d),
                     attn_lam_q[0], attn_lam_k[0], attn_subln_g[0], lam_init)
    past = cache_k.shape[2]
    k_all = jnp.concatenate([k_l.reshape(b_lat, l_lat, d), cache_k[:, 0].reshape(b_lat, past, d).astype(BF16)], axis=1)
    v_all = jnp.concatenate([v_l.reshape(b_lat, l_lat, d), cache_v[:, 0].reshape(b_lat, past, d).astype(BF16)], axis=1)
    o_l = _attention(q_l.reshape(b_lat, l_lat, d), k_all, v_all,
                     attn_lam_q[0], attn_lam_k[0], attn_subln_g[0], lam_init)
    o = jnp.concatenate([o_l.reshape(n_lat, d), o_c.reshape(n_ctx, d)], axis=0)
    x = _oproj(x, o, mod[1], attn_w_o[0].astype(BF16), cmap)
    y = _moe(x, mod[1], 1, (cmap_r, cmap_c), p, final=True)

    return (y[n_lat:].reshape(b_ctx, l_ctx, d), y[:n_lat].reshape(b_lat, l_lat, d),
            new_state, new_k, new_v)
```

```python
import functools
import math

import jax
import jax.numpy as jnp
from jax import lax
from jax.experimental import pallas as pl
from jax.experimental.pallas import tpu as pltpu

F32 = jnp.float32
BF16 = jnp.bfloat16
I32 = jnp.int32
U32 = jnp.uint32

EPS = 1e-6
SSM_GROUP = 16
CHUNK = 16
GRID_W = 64
HEAD_DIM = 64
V_DIM = 2 * HEAD_DIM
ROPE_THETA = 10000.0
TOP_K = 8
N_EXPERT_GROUPS = 8
TOPK_GROUPS = 4
ROUTED_SCALE = 2.5
N_MOD = 6

LANES = 128
VMEM_LIMIT = 48 * 1024 * 1024

TM_DENSE = 512
TM_ROUTER = 256
TM_EXPERT = 512
TM_ROWS = 256
TQ_ATTN = 256
DMA_PRIORITIES = 2


def _params(sem):
    return pltpu.CompilerParams(dimension_semantics=sem, vmem_limit_bytes=VMEM_LIMIT)


def _normmod(x, g, sc, sh):
    ms = jnp.mean(x * x, axis=-1, keepdims=True)
    return (x * lax.rsqrt(ms + EPS) * g) * (1.0 + sc) + sh


def _cond_map(tm, n_lat, dec_seq):
    t_lat = n_lat // tm
    per = dec_seq // tm

    def f(i):
        return jnp.where(i < t_lat, 1 + i // per, 0)
    return f


def _ada_kernel(c_ref, w_ref, b_ref, o_ref):
    c = c_ref[...]
    s = (c * jax.nn.sigmoid(c)).astype(BF16)
    o_ref[0] = jnp.dot(s, w_ref[0].astype(BF16), preferred_element_type=F32) + b_ref[0]


def _ada(cond, ada_w, ada_b):
    depth, d, n6 = ada_w.shape
    nc = cond.shape[0]
    tn = 1536
    out = pl.pallas_call(
        _ada_kernel,
        grid=(depth, n6 // tn),
        in_specs=[pl.BlockSpec((nc, d), lambda l, j: (0, 0)),
                  pl.BlockSpec((1, d, tn), lambda l, j: (l, 0, j)),
                  pl.BlockSpec((1, 1, tn), lambda l, j: (l, 0, j))],
        out_specs=pl.BlockSpec((1, nc, tn), lambda l, j: (l, 0, j)),
        out_shape=jax.ShapeDtypeStruct((depth, nc, n6), F32),
        compiler_params=_params(("arbitrary", "arbitrary")),
        name="ada",
    )(cond, ada_w, ada_b.reshape(depth, 1, n6))
    return out.reshape(depth, nc, N_MOD, d)


GROUPS_PER_SLAB = LANES // SSM_GROUP
HALF_CHUNK = CHUNK // 2


def _chunk_perm():
    r = jnp.arange(HALF_CHUNK * LANES)
    t8, g8, c = r // LANES, (r // SSM_GROUP) % GROUPS_PER_SLAB, r % SSM_GROUP
    col = g8 * LANES + t8 * SSM_GROUP + c
    return (col[:, None] == r[None, :]).astype(BF16)


def _two_sources(tm, d, tiles_a):
    last_a = tiles_a - 1
    return [pl.BlockSpec((tm, d), lambda i: (jnp.minimum(i, last_a), 0)),
            pl.BlockSpec((tm, d), lambda i: (jnp.maximum(i - tiles_a, 0), 0))]


def _pick(a_ref, b_ref, tiles_a):
    return jnp.where(pl.program_id(0) < tiles_a, a_ref[...], b_ref[...])


def _pack_kernel(xa_ref, xb_ref, mod_ref, g_ref, p_ref, o_ref, h_s, *, tiles_a):
    m = mod_ref[0]
    h = _normmod(_pick(xa_ref, xb_ref, tiles_a), g_ref[...], m[1:2], m[0:1])
    tm, d = h.shape
    nct = tm // CHUNK
    for j in range(d // LANES):
        h_s[j] = h[:, j * LANES:(j + 1) * LANES]
    slabs = []
    for j in range(d // LANES):
        for hf in range(2):
            slabs.append(jnp.concatenate(
                [h_s[j, pl.ds(hf * HALF_CHUNK + t8, nct, stride=CHUNK), :] for t8 in range(HALF_CHUNK)], axis=1))
    lhs = jnp.concatenate(slabs, axis=0).astype(BF16)
    out = jnp.dot(lhs, p_ref[...], preferred_element_type=F32)
    kw = CHUNK * SSM_GROUP
    for j in range(d // LANES):
        for hf in range(2):
            blk = out[(2 * j + hf) * nct:(2 * j + hf + 1) * nct]
            for g8 in range(GROUPS_PER_SLAB):
                col = (j * GROUPS_PER_SLAB + g8) * kw + hf * LANES
                o_ref[:, col:col + LANES] = blk[:, g8 * LANES:(g8 + 1) * LANES]


def _s5_pack(x_a, x_b, mod, g, cmap):
    d = x_a.shape[1]
    n = x_a.shape[0] + x_b.shape[0]
    tm = TM_DENSE
    tiles_a = x_a.shape[0] // tm
    return pl.pallas_call(
        functools.partial(_pack_kernel, tiles_a=tiles_a),
        grid=(n // tm,),
        in_specs=_two_sources(tm, d, tiles_a) + [
                  pl.BlockSpec((1, N_MOD, d), lambda i: (cmap(i), 0, 0)),
                  pl.BlockSpec((1, d), lambda i: (0, 0)),
                  pl.BlockSpec((HALF_CHUNK * LANES, HALF_CHUNK * LANES), lambda i: (0, 0))],
        out_specs=pl.BlockSpec((tm // CHUNK, CHUNK * d), lambda i: (i, 0)),
        out_shape=jax.ShapeDtypeStruct((n // CHUNK, CHUNK * d), F32),
        scratch_shapes=[pltpu.VMEM((d // LANES, tm, LANES), F32)],
        compiler_params=_params(("arbitrary",)),
        name="s5_pack",
    )(x_a, x_b, mod, g.reshape(1, d), _chunk_perm())


def _s5_weights(lam_re, lam_im, log_dt, b_re, b_im, c_re, c_im):
    hi = lax.Precision.HIGHEST
    t = CHUNK
    w_in, w_out, toep, a16 = [], [], [], []
    for dr in range(2):
        dt = jnp.exp(log_dt[dr])[:, None]
        lr, li = lam_re[dr], lam_im[dr]
        mag = jnp.exp(lr * dt)
        a_re = mag * jnp.cos(li * dt)
        a_im = mag * jnp.sin(li * dt)
        den = lr * lr + li * li
        f_re = ((a_re - 1.0) * lr + a_im * li) / den
        f_im = (a_im * lr - (a_re - 1.0) * li) / den
        bb_re = f_re[..., None] * b_re[dr] - f_im[..., None] * b_im[dr]
        bb_im = f_re[..., None] * b_im[dr] + f_im[..., None] * b_re[dr]
        k = jnp.arange(t + 1, dtype=F32)[:, None, None]
        pm = jnp.exp(k * (lr * dt)[None])
        pk_re = pm * jnp.cos(k * (li * dt)[None])
        pk_im = pm * jnp.sin(k * (li * dt)[None])
        ks = jnp.arange(t)
        sel = (t - 1 - ks) if dr == 0 else ks
        pr, pi = pk_re[sel], pk_im[sel]
        wi_re = pr[..., None] * bb_re[None] - pi[..., None] * bb_im[None]
        wi_im = pr[..., None] * bb_im[None] + pi[..., None] * bb_re[None]
        g_, p_, c_ = bb_re.shape
        w_in.append((wi_re.transpose(1, 0, 3, 2).reshape(g_, t * c_, p_),
                     wi_im.transpose(1, 0, 3, 2).reshape(g_, t * c_, p_)))
        sel = (ks + 1) if dr == 0 else (t - ks)
        pr, pi = pk_re[sel], pk_im[sel]
        ca_re = c_re[dr][None] * pr[:, :, None, :] - c_im[dr][None] * pi[:, :, None, :]
        ca_im = c_re[dr][None] * pi[:, :, None, :] + c_im[dr][None] * pr[:, :, None, :]
        w_out.append((ca_re.transpose(1, 3, 0, 2).reshape(g_, p_, t * c_),
                      (-ca_im).transpose(1, 3, 0, 2).reshape(g_, p_, t * c_)))
        pr, pi = pk_re[:t], pk_im[:t]
        cd_re = c_re[dr][None] * pr[:, :, None, :] - c_im[dr][None] * pi[:, :, None, :]
        cd_im = c_re[dr][None] * pi[:, :, None, :] + c_im[dr][None] * pr[:, :, None, :]
        kk = jnp.einsum('kgop,gpi->kgoi', jnp.concatenate([cd_re, -cd_im], axis=3),
                        jnp.concatenate([bb_re, bb_im], axis=1), precision=hi)
        s_i = ks[:, None]
        t_i = ks[None, :]
        delta = (t_i - s_i) if dr == 0 else (s_i - t_i)
        kd = jnp.where((delta >= 0)[:, :, None, None, None], kk[jnp.clip(delta, 0, t - 1)], 0.0)
        toep.append(kd.transpose(2, 0, 4, 1, 3).reshape(g_, t * c_, t * c_))
        a16.append((pk_re[t], pk_im[t]))
    w_in_all = jnp.concatenate([w_in[0][0], w_in[1][0], w_in[0][1], w_in[1][1]], axis=2)
    w_out_all = jnp.concatenate([w_out[0][0], w_out[1][0], w_out[0][1], w_out[1][1]], axis=1)
    toep_all = toep[0] + toep[1]
    a16_all = jnp.stack([jnp.concatenate([a16[0][0], a16[1][0]], axis=1),
                         jnp.concatenate([a16[0][1], a16[1][1]], axis=1)], axis=1)
    return w_in_all.astype(BF16), toep_all.astype(BF16), w_out_all.astype(BF16), a16_all


def _s5_kernel(u_hbm, win_ref, wtp_ref, wout_ref, a16_ref, h0_ref, y_hbm, fin_ref,
               ubuf, ybuf, v_s, sf_s, sb_s, sem_in, sem_out, *, nc, nb, row0):
    g = pl.program_id(0)
    ng = pl.num_programs(0)
    kw = ubuf.shape[3]
    slot = g % 2

    def in_copy(grp, s, b):
        return pltpu.make_async_copy(u_hbm.at[pl.ds(row0 + b * nc, nc), pl.ds(grp * kw, kw)],
                                     ubuf.at[s, :, b, :], sem_in.at[s])

    def out_copy(grp, s, b):
        return pltpu.make_async_copy(ybuf.at[s, :, b, :],
                                     y_hbm.at[pl.ds(b * nc, nc), pl.ds(grp * kw, kw)], sem_out.at[s])

    @pl.when(g == 0)
    def _():
        for b in range(nb):
            in_copy(g, slot, b).start()

    @pl.when(g + 1 < ng)
    def _():
        for b in range(nb):
            in_copy(g + 1, 1 - slot, b).start()

    for b in range(nb):
        in_copy(g, slot, b).wait()
    u = ubuf[slot].reshape(nc * nb, kw).astype(BF16)
    v_s[...] = jnp.dot(u, win_ref[0], preferred_element_type=F32)
    a = a16_ref[0]
    ar, ai = a[0:1], a[1:2]
    fwd = (lax.broadcasted_iota(I32, (nb, 2 * LANES), 1) % LANES) < (LANES // 2)

    def step(i, carry):
        sr, si = carry
        st = jnp.concatenate([sr, si], axis=1)
        rf = pl.multiple_of(i * nb, nb)
        rb = pl.multiple_of((nc - 1 - i) * nb, nb)
        sf_s[pl.ds(rf, nb), :] = st
        sb_s[pl.ds(rb, nb), :] = st
        v = jnp.where(fwd, v_s[pl.ds(rf, nb), :], v_s[pl.ds(rb, nb), :])
        return ar * sr - ai * si + v[:, :LANES], ar * si + ai * sr + v[:, LANES:]

    h0 = h0_ref[0]
    sr, si = lax.fori_loop(0, nc, step, (h0[:, :LANES], h0[:, LANES:]))
    fin_ref[0] = jnp.concatenate([sr, si], axis=1)
    fwd_all = (lax.broadcasted_iota(I32, (nc * nb, 2 * LANES), 1) % LANES) < (LANES // 2)
    hs = jnp.where(fwd_all, sf_s[...], sb_s[...]).astype(BF16)
    y = (jnp.dot(u, wtp_ref[0], preferred_element_type=F32)
         + jnp.dot(hs, wout_ref[0], preferred_element_type=F32))

    @pl.when(g >= 2)
    def _():
        for b in range(nb):
            out_copy(g - 2, slot, b).wait()

    ybuf[slot] = y.reshape(nc, nb, kw)
    for b in range(nb):
        out_copy(g, slot, b).start()

    @pl.when(g == ng - 1)
    def _():
        if ng > 1:
            for b in range(nb):
                out_copy(g - 1, 1 - slot, b).wait()
        for b in range(nb):
            out_copy(g, slot, b).wait()


def _s5(u, h0, wts, row0, nb, nc):
    w_in, w_tp, w_out, a16 = wts
    r = nb * nc
    kw = CHUNK * SSM_GROUP
    g = u.shape[1] // kw
    return pl.pallas_call(
        functools.partial(_s5_kernel, nc=nc, nb=nb, row0=row0),
        grid=(g,),
        in_specs=[pl.BlockSpec(memory_space=pl.ANY),
                  pl.BlockSpec((1, kw, kw), lambda j: (j, 0, 0)),
                  pl.BlockSpec((1, kw, kw), lambda j: (j, 0, 0)),
                  pl.BlockSpec((1, kw, kw), lambda j: (j, 0, 0)),
                  pl.BlockSpec((1, 2, LANES), lambda j: (j, 0, 0)),
                  pl.BlockSpec((1, nb, kw), lambda j: (j, 0, 0))],
        out_specs=[pl.BlockSpec(memory_space=pl.ANY),
                   pl.BlockSpec((1, nb, kw), lambda j: (j, 0, 0))],
        out_shape=[jax.ShapeDtypeStruct((r, u.shape[1]), F32),
                   jax.ShapeDtypeStruct((g, nb, kw), F32)],
        scratch_shapes=[pltpu.VMEM((2, nc, nb, kw), F32), pltpu.VMEM((2, nc, nb, kw), F32),
                        pltpu.VMEM((r, kw), F32), pltpu.VMEM((r, kw), F32), pltpu.VMEM((r, kw), F32),
                        pltpu.SemaphoreType.DMA((2,)), pltpu.SemaphoreType.DMA((2,))],
        compiler_params=_params(("arbitrary",)),
        name="s5_scan",
    )(u, w_in, w_tp, w_out, a16, h0)


def _glu_kernel(xa_ref, xb_ref, ya_ref, yb_ref, mod_ref, g_ref, d_ref, pt_ref, w_ref, o_ref, y_s, *, tiles_a):
    tm, dd = xa_ref.shape
    nct = tm // CHUNK
    kw = CHUNK * SSM_GROUP
    from_a = pl.program_id(0) < tiles_a
    slabs = []
    for j in range(dd // LANES):
        for hf in range(2):
            cols = [(j * GROUPS_PER_SLAB + g8) * kw + hf * LANES for g8 in range(GROUPS_PER_SLAB)]
            slabs.append(jnp.concatenate(
                [jnp.where(from_a, ya_ref[:, c0:c0 + LANES], yb_ref[:, c0:c0 + LANES]) for c0 in cols], axis=1))
    lhs = jnp.concatenate(slabs, axis=0)
    hi = lhs.astype(BF16)
    lo = (lhs - hi.astype(F32)).astype(BF16)
    pt = pt_ref[...]
    nat = jnp.dot(hi, pt, preferred_element_type=F32) + jnp.dot(lo, pt, preferred_element_type=F32)
    for j in range(dd // LANES):
        for hf in range(2):
            blk = nat[(2 * j + hf) * nct:(2 * j + hf + 1) * nct]
            for t8 in range(HALF_CHUNK):
                y_s[j, pl.ds(hf * HALF_CHUNK + t8, nct, stride=CHUNK), :] = blk[:, t8 * LANES:(t8 + 1) * LANES]
    m = mod_ref[0]
    x = _pick(xa_ref, xb_ref, tiles_a)
    h = _normmod(x, g_ref[...], m[1:2], m[0:1])
    y = d_ref[...] * h + jnp.concatenate([y_s[j] for j in range(dd // LANES)], axis=1)
    gl = jax.nn.gelu(y).astype(BF16)
    z = jnp.dot(gl, w_ref[...], preferred_element_type=F32)
    o_ref[...] = x + m[2:3] * (z[:, :dd] * jax.nn.sigmoid(z[:, dd:]))


def _glu(x_a, x_b, y_a, y_b, mod, g, dskip, w, cmap):
    d = x_a.shape[1]
    n = x_a.shape[0] + x_b.shape[0]
    tm = TM_DENSE
    nct = tm // CHUNK
    tiles_a = y_a.shape[0] // nct
    assert tiles_a == x_a.shape[0] // tm
    last_a = tiles_a - 1
    return pl.pallas_call(
        functools.partial(_glu_kernel, tiles_a=tiles_a),
        grid=(n // tm,),
        in_specs=_two_sources(tm, d, tiles_a) + [
                  pl.BlockSpec((nct, CHUNK * d), lambda i: (jnp.minimum(i, last_a), 0)),
                  pl.BlockSpec((nct, CHUNK * d), lambda i: (jnp.maximum(i - tiles_a, 0), 0)),
                  pl.BlockSpec((1, N_MOD, d), lambda i: (cmap(i), 0, 0)),
                  pl.BlockSpec((1, d), lambda i: (0, 0)),
                  pl.BlockSpec((1, d), lambda i: (0, 0)),
                  pl.BlockSpec((HALF_CHUNK * LANES, HALF_CHUNK * LANES), lambda i: (0, 0)),
                  pl.BlockSpec((d, 2 * d), lambda i: (0, 0))],
        out_specs=pl.BlockSpec((tm, d), lambda i: (i, 0)),
        out_shape=jax.ShapeDtypeStruct((n, d), F32),
        scratch_shapes=[pltpu.VMEM((d // LANES, tm, LANES), F32)],
        compiler_params=_params(("arbitrary",)),
        name="s5_glu",
    )(x_a, x_b, y_a, y_b, mod, g.reshape(1, d), dskip.reshape(1, d), _chunk_perm().T, w)


def _rope_tables(l):
    r = HEAD_DIM // 2
    half = r // 2
    freq = ROPE_THETA ** (-jnp.arange(half, dtype=F32) / half)
    pos = jnp.arange(l)
    row = (pos // GRID_W).astype(F32)
    col = (pos % GRID_W).astype(F32)
    d = jnp.arange(LANES) % HEAD_DIM
    p = jnp.where((d < r)[None, :], row[:, None], col[:, None])
    ang = p * freq[d % half][None, :]
    sign = jnp.where((d % r) < half, -1.0, 1.0)[None, :]
    return jnp.cos(ang), jnp.sin(ang) * sign


def _qkv_kernel(*refs, rope):
    if rope:
        x_ref, mod_ref, g_ref, w_ref, cos_ref, sin_ref, q_ref, k_ref, v_ref = refs
    else:
        x_ref, mod_ref, g_ref, w_ref, q_ref, k_ref, v_ref, kf_ref, vf_ref = refs
    m = mod_ref[0]
    h = _normmod(x_ref[...], g_ref[...], m[1:2], m[0:1]).astype(BF16)
    qkv = jnp.dot(h, w_ref[...], preferred_element_type=F32)
    d = x_ref.shape[1]
    q = qkv[:, :d] * (HEAD_DIM ** -0.5 * math.log2(math.e))
    k = qkv[:, d:2 * d]
    v = qkv[:, 2 * d:]
    v_ref[...] = v.astype(BF16)
    if not rope:
        q_ref[...] = q.astype(BF16)
        k_ref[...] = k.astype(BF16)
        kf_ref[...] = k
        vf_ref[...] = v
        return
    cos = cos_ref[...]
    sin = sin_ref[...]
    half = HEAD_DIM // 4
    lane = lax.broadcasted_iota(I32, cos.shape, 1)
    up_sel = (lane % (2 * half)) < half
    for src, dst in ((q, q_ref), (k, k_ref)):
        for j in range(d // LANES):
            blk = src[:, j * LANES:(j + 1) * LANES]
            up = pltpu.roll(blk, LANES - half, 1)
            dn = pltpu.roll(blk, half, 1)
            dst[:, j * LANES:(j + 1) * LANES] = (blk * cos + jnp.where(up_sel, up, dn) * sin).astype(BF16)


def _qkv(x, mod, g, w, cmap, row0, nrows, rope_l=None):
    n, d = x.shape
    tm = TM_DENSE
    t0 = row0 // tm
    rope = rope_l is not None
    in_specs = [pl.BlockSpec((tm, d), lambda i: (i + t0, 0)),
                pl.BlockSpec((1, N_MOD, d), lambda i: (cmap(i + t0), 0, 0)),
                pl.BlockSpec((1, d), lambda i: (0, 0)),
                pl.BlockSpec((d, 3 * d), lambda i: (0, 0))]
    args = [x, mod, g.reshape(1, d), w]
    row_spec = pl.BlockSpec((tm, d), lambda i: (i, 0))
    out_specs = [row_spec] * 3
    out_shape = [jax.ShapeDtypeStruct((nrows, d), BF16)] * 3
    if rope:
        per = rope_l // tm
        cos, sin = _rope_tables(rope_l)
        in_specs += [pl.BlockSpec((tm, LANES), lambda i: (i % per, 0))] * 2
        args += [cos, sin]
    else:
        out_specs += [row_spec] * 2
        out_shape += [jax.ShapeDtypeStruct((nrows, d), F32)] * 2
    return pl.pallas_call(
        functools.partial(_qkv_kernel, rope=rope),
        grid=(nrows // tm,),
        in_specs=in_specs,
        out_specs=out_specs,
        out_shape=out_shape,
        compiler_params=_params(("arbitrary",)),
        name="qkv_rope" if rope else "qkv",
    )(*args)


def _attn_kernel(q_ref, kt_ref, v_ref, lq_ref, lk_ref, g_ref, o_ref, *, lam_init):
    q = q_ref[0]
    tq = q.shape[0]
    lane = lax.broadcasted_iota(I32, q.shape, 1)
    zero = jnp.zeros_like(q)
    qq = jnp.concatenate([jnp.where(lane < HEAD_DIM, q, zero), jnp.where(lane >= HEAD_DIM, q, zero)], axis=0)
    s = jnp.dot(qq, kt_ref[0, 0], preferred_element_type=F32)
    p = jnp.exp2(s - jnp.max(s, axis=1, keepdims=True))
    l = jnp.sum(p, axis=1, keepdims=True)
    lqk = lq_ref[...] * lk_ref[...]
    lam = (jnp.exp(jnp.sum(lqk[0:1], axis=1, keepdims=True))
           - jnp.exp(jnp.sum(lqk[1:2], axis=1, keepdims=True)) + lam_init)
    w = p[:tq] * (1.0 / l[:tq]) - p[tq:] * (lam / l[tq:])
    o = jnp.dot(w.astype(BF16), v_ref[0], preferred_element_type=F32)
    o = o * lax.rsqrt(jnp.mean(o * o, axis=1, keepdims=True) + EPS) * g_ref[...]
    o_ref[0] = (o * (1.0 - lam_init)).astype(o_ref.dtype)


def _attention(q, k, v, lam_q, lam_k, subln_g, lam_init):
    nb, l, d = q.shape
    lk = k.shape[1]
    nh = d // V_DIM
    kt = k.reshape(nb, lk, nh, V_DIM).transpose(0, 2, 3, 1)
    tq = min(TQ_ATTN, l)
    return pl.pallas_call(
        functools.partial(_attn_kernel, lam_init=lam_init),
        grid=(nb, nh, l // tq),
        in_specs=[pl.BlockSpec((1, tq, V_DIM), lambda b, h, i: (b, i, h)),
                  pl.BlockSpec((1, 1, V_DIM, lk), lambda b, h, i: (b, h, 0, 0)),
                  pl.BlockSpec((1, lk, V_DIM), lambda b, h, i: (b, 0, h)),
                  pl.BlockSpec((2, HEAD_DIM), lambda b, h, i: (0, 0)),
                  pl.BlockSpec((2, HEAD_DIM), lambda b, h, i: (0, 0)),
                  pl.BlockSpec((1, V_DIM), lambda b, h, i: (0, 0))],
        out_specs=pl.BlockSpec((1, tq, V_DIM), lambda b, h, i: (b, i, h)),
        out_shape=jax.ShapeDtypeStruct((nb, l, d), BF16),
        compiler_params=_params(("arbitrary", "arbitrary", "arbitrary")),
        name="diff_attn",
    )(q, kt, v, lam_q, lam_k, subln_g.reshape(1, V_DIM))


def _oproj_kernel(x_ref, oa_ref, ob_ref, mod_ref, w_ref, y_ref, *, tiles_a):
    m = mod_ref[0]
    o = _pick(oa_ref, ob_ref, tiles_a)
    y_ref[...] = x_ref[...] + m[2:3] * jnp.dot(o, w_ref[...], preferred_element_type=F32)


def _oproj(x, o_a, o_b, mod, w, cmap):
    n, d = x.shape
    tm = TM_DENSE
    tiles_a = o_a.shape[0] // tm
    return pl.pallas_call(
        functools.partial(_oproj_kernel, tiles_a=tiles_a),
        grid=(n // tm,),
        in_specs=[pl.BlockSpec((tm, d), lambda i: (i, 0))] + _two_sources(tm, d, tiles_a) + [
                  pl.BlockSpec((1, N_MOD, d), lambda i: (cmap(i), 0, 0)),
                  pl.BlockSpec((d, d), lambda i: (0, 0))],
        out_specs=pl.BlockSpec((tm, d), lambda i: (i, 0)),
        out_shape=jax.ShapeDtypeStruct((n, d), F32),
        compiler_params=_params(("arbitrary",)),
        name="attn_out",
    )(x, o_a, o_b, mod, w)


def _pack_rows(h):
    half = h.shape[1] // 2
    bits = lax.bitcast_convert_type(h.astype(BF16).astype(F32), U32)
    return bits[:, half:] | (bits[:, :half] >> 16)


def _unpack_rows(w):
    lo = lax.bitcast_convert_type(w << 16, F32)
    hi = lax.bitcast_convert_type(w & jnp.uint32(0xFFFF0000), F32)
    return lo, hi


def _stack_rows(rows, width):
    ri = lax.broadcasted_iota(I32, (len(rows), width), 0)
    out = jnp.zeros((len(rows), width), rows[0].dtype)
    for j, r in enumerate(rows):
        out = jnp.where(ri == j, r, out)
    return out


def _router_kernel(x_ref, mod_ref, g_ref, wh_ref, wl_ref, b_ref, h_ref, idx_ref, w_ref, pos_ref, cnt_ref, cnt_s):
    step = pl.program_id(0)

    @pl.when(step == 0)
    def _():
        cnt_s[...] = jnp.zeros_like(cnt_s)

    m = mod_ref[0]
    h = _normmod(x_ref[...], g_ref[...], m[4:5], m[3:4])
    h_ref[...] = _pack_rows(h)
    tr = h.shape[0]
    ne = wh_ref.shape[0]
    per = ne // N_EXPERT_GROUPS
    hh = h.astype(BF16)
    hl = (h - hh.astype(F32)).astype(BF16)
    nt = (((1,), (1,)), ((), ()))
    wh = wh_ref[...]
    logits = (lax.dot_general(wh, hh, nt, preferred_element_type=F32)
              + lax.dot_general(wl_ref[...], hh, nt, preferred_element_type=F32)
              + lax.dot_general(wh, hl, nt, preferred_element_type=F32))
    scores = jax.nn.sigmoid(logits)
    biased = scores + b_ref[...]
    neg = -jnp.inf
    ri = lax.broadcasted_iota(I32, (per, tr), 0).astype(F32)
    blocks, gscores = [], []
    for j in range(N_EXPERT_GROUPS):
        blk = biased[j * per:(j + 1) * per, :]
        m1 = jnp.max(blk, axis=0, keepdims=True)
        f1 = jnp.min(jnp.where(blk == m1, ri, float(per)), axis=0, keepdims=True)
        m2 = jnp.max(jnp.where(ri == f1, neg, blk), axis=0, keepdims=True)
        blocks.append(blk)
        gscores.append(m1 + m2)
    gs = _stack_rows(gscores, tr)
    gi = lax.broadcasted_iota(I32, gs.shape, 0).astype(F32)
    gsel = jnp.zeros(gs.shape, F32)
    for _ in range(TOPK_GROUPS):
        mm = jnp.max(gs, axis=0, keepdims=True)
        pick = gi == jnp.min(jnp.where(gs == mm, gi, float(N_EXPERT_GROUPS)), axis=0, keepdims=True)
        gsel = jnp.where(pick, 1.0, gsel)
        gs = jnp.where(pick, neg, gs)
    masked = jnp.concatenate(
        [jnp.where(gsel[j:j + 1, :] > 0.0, blocks[j], neg) for j in range(N_EXPERT_GROUPS)], axis=0)
    ei = lax.broadcasted_iota(I32, masked.shape, 0).astype(F32)
    picks, idxs, ws = [], [], []
    for _ in range(TOP_K):
        mm = jnp.max(masked, axis=0, keepdims=True)
        ff = jnp.min(jnp.where(masked == mm, ei, float(ne)), axis=0, keepdims=True)
        pick = ei == ff
        picks.append(pick)
        idxs.append(ff)
        ws.append(jnp.sum(jnp.where(pick, scores, 0.0), axis=0, keepdims=True))
        masked = jnp.where(pick, neg, masked)
    wsum = ws[0]
    for wk in ws[1:]:
        wsum = wsum + wk
    w_ref[...] = _stack_rows([wk / wsum * ROUTED_SCALE for wk in ws], tr)
    idx_ref[...] = _stack_rows(idxs, tr).astype(I32)
    sel = jnp.zeros(masked.shape, F32)
    for pick in picks:
        sel = sel + jnp.where(pick, 1.0, 0.0)
    earlier = (lax.broadcasted_iota(I32, (tr, tr), 0) < lax.broadcasted_iota(I32, (tr, tr), 1))
    prefix = jnp.dot(sel.astype(BF16), jnp.where(earlier, 1.0, 0.0).astype(BF16), preferred_element_type=F32)
    base = prefix + cnt_s[...]
    pos_ref[...] = _stack_rows(
        [jnp.sum(jnp.where(pick, base, 0.0), axis=0, keepdims=True) for pick in picks], tr).astype(I32)
    cnt_s[...] = cnt_s[...] + jnp.sum(sel, axis=1, keepdims=True)
    cnt_ref[...] = cnt_s[...]


def _router(x, mod, g, w_t_hi, w_t_lo, bias, cmap):
    n, d = x.shape
    ne = w_t_hi.shape[0]
    tr = TM_ROUTER
    tok_spec = pl.BlockSpec((TOP_K, tr), lambda i: (0, i))
    return pl.pallas_call(
        _router_kernel,
        grid=(n // tr,),
        in_specs=[pl.BlockSpec((tr, d), lambda i: (i, 0)),
                  pl.BlockSpec((1, N_MOD, d), lambda i: (cmap(i), 0, 0)),
                  pl.BlockSpec((1, d), lambda i: (0, 0)),
                  pl.BlockSpec((ne, d), lambda i: (0, 0)),
                  pl.BlockSpec((ne, d), lambda i: (0, 0)),
                  pl.BlockSpec((ne, 1), lambda i: (0, 0))],
        out_specs=[pl.BlockSpec((tr, d // 2), lambda i: (i, 0)), tok_spec, tok_spec, tok_spec,
                   pl.BlockSpec((ne, 1), lambda i: (0, 0))],
        out_shape=[jax.ShapeDtypeStruct((n, d // 2), U32),
                   jax.ShapeDtypeStruct((TOP_K, n), I32),
                   jax.ShapeDtypeStruct((TOP_K, n), F32),
                   jax.ShapeDtypeStruct((TOP_K, n), I32),
                   jax.ShapeDtypeStruct((ne, 1), F32)],
        scratch_shapes=[pltpu.VMEM((ne, 1), F32)],
        compiler_params=_params(("arbitrary",)),
        name="moe_router",
    )(x, mod, g.reshape(1, d), w_t_hi, w_t_lo, bias.reshape(ne, 1))


def _dest_kernel(idx_ref, pos_ref, start_ref, o_ref):
    idx = idx_ref[...]
    ei = lax.broadcasted_iota(I32, (start_ref.shape[0], idx.shape[1]), 0)
    start = start_ref[...]
    rows = [jnp.sum(jnp.where(ei == idx[k:k + 1], start, 0.0), axis=0, keepdims=True) for k in range(TOP_K)]
    o_ref[...] = _stack_rows(rows, idx.shape[1]).astype(I32) + pos_ref[...]


def _dest(idx, pos, start):
    n = idx.shape[1]
    ne = start.shape[0]
    tr = 512
    tok_spec = pl.BlockSpec((TOP_K, tr), lambda i: (0, i))
    return pl.pallas_call(
        _dest_kernel,
        grid=(n // tr,),
        in_specs=[tok_spec, tok_spec, pl.BlockSpec((ne, 1), lambda i: (0, 0))],
        out_specs=tok_spec,
        out_shape=jax.ShapeDtypeStruct((TOP_K, n), I32),
        compiler_params=_params(("arbitrary",)),
        name="moe_dest",
    )(idx, pos, start)


def _dispatch_kernel(dest_ref, h_ref, xs_ref, sem):
    tm = h_ref.shape[0]

    def issue(t, carry):
        for k in range(TOP_K):
            pltpu.make_async_copy(h_ref.at[pl.ds(t, 1), :], xs_ref.at[pl.ds(dest_ref[k, t], 1), :],
                                  sem).start(priority=k % DMA_PRIORITIES)
        return carry

    lax.fori_loop(0, tm, issue, 0)
    for _ in range(TOP_K):
        pltpu.make_async_copy(h_ref, xs_ref.at[pl.ds(0, tm), :], sem).wait()


def _dispatch(h, dest):
    n, d = h.shape
    tm = TM_ROWS
    return pl.pallas_call(
        _dispatch_kernel,
        grid=(n // tm,),
        in_specs=[pl.BlockSpec((TOP_K, tm), lambda i: (0, i), memory_space=pltpu.SMEM),
                  pl.BlockSpec((tm, d), lambda i: (i, 0))],
        out_specs=pl.BlockSpec(memory_space=pl.ANY),
        out_shape=jax.ShapeDtypeStruct((n * TOP_K, d), U32),
        scratch_shapes=[pltpu.SemaphoreType.DMA(())],
        compiler_params=_params(("arbitrary",)),
        name="moe_dispatch",
    )(dest, h)


def _expert_kernel(tile_ref, exp_ref, lo_ref, hi_ref, newe_ref, newt_ref, last_ref,
                   xs_ref, w1_ref, w3_ref, w2_ref, ys_ref, w1_s, w3_s, w2_s, acc_s):
    i = pl.program_id(0)

    @pl.when(newe_ref[i] == 1)
    def _():
        w1_s[...] = w1_ref[0, 0].astype(BF16)
        w3_s[...] = w3_ref[0, 0].astype(BF16)
        w2_s[...] = w2_ref[0, 0].astype(BF16)

    tm = xs_ref.shape[0]
    lo, hi = lo_ref[i], hi_ref[i]

    @pl.when(hi > lo)
    def _():
        x_lo, x_hi = _unpack_rows(xs_ref[...])
        x = jnp.concatenate([x_lo, x_hi], axis=1).astype(BF16)
        a = jnp.dot(x, w1_s[...], preferred_element_type=F32)
        b = jnp.dot(x, w3_s[...], preferred_element_type=F32)
        row = tile_ref[i] * tm + lax.broadcasted_iota(I32, (tm, 1), 0)
        hb = jnp.where((row >= lo) & (row < hi), a * jax.nn.sigmoid(a) * b, 0.0).astype(BF16)
        y = jnp.dot(hb, w2_s[...], preferred_element_type=F32)
        first, last = newt_ref[i] == 1, last_ref[i] == 1

        @pl.when(first & last)
        def _():
            ys_ref[...] = _pack_rows(y)

        @pl.when(first & jnp.logical_not(last))
        def _():
            acc_s[...] = y

        @pl.when(jnp.logical_not(first))
        def _():
            acc_s[...] = acc_s[...] + y

    @pl.when((last_ref[i] == 1) & jnp.logical_not((newt_ref[i] == 1) & (hi > lo)))
    def _():
        ys_ref[...] = _pack_rows(acc_s[...])


def _work_items(counts, n_rows, tm):
    ne = counts.shape[0]
    n_tiles = n_rows // tm
    n_items = n_tiles + ne - 1
    end = jnp.cumsum(counts)
    start = end - counts
    first_t = start // tm
    n_e = jnp.where(counts > 0, (end - 1) // tm - first_t + 1, 0)
    item_end = jnp.cumsum(n_e)
    item_off = item_end - n_e
    it = jnp.arange(n_items, dtype=I32)
    total = item_end[-1]
    valid = it < total
    last_item = jnp.minimum(it, total - 1)
    e = jnp.minimum(jnp.sum((item_end[None, :] <= last_item[:, None]).astype(I32), axis=1), ne - 1).astype(I32)
    tile = jnp.where(valid, first_t[e] + it - item_off[e], n_tiles - 1).astype(I32)
    lo = jnp.where(valid, jnp.maximum(start[e], tile * tm), 0).astype(I32)
    hi = jnp.where(valid, jnp.minimum(end[e], (tile + 1) * tm), 0).astype(I32)
    prev_e = jnp.concatenate([jnp.full((1,), -1, I32), e[:-1]])
    prev_t = jnp.concatenate([jnp.full((1,), -1, I32), tile[:-1]])
    next_t = jnp.concatenate([tile[1:], jnp.full((1,), -1, I32)])
    return (tile, e, lo, hi, (e != prev_e).astype(I32), (tile != prev_t).astype(I32),
            (tile != next_t).astype(I32))


def _experts(xs, items, w1, w3, w2, layer):
    n_rows, dw = xs.shape
    d, de = w1.shape[2], w1.shape[3]
    tm = TM_EXPERT
    n_items = items[0].shape[0]
    grid_spec = pltpu.PrefetchScalarGridSpec(
        num_scalar_prefetch=len(items),
        grid=(n_items,),
        in_specs=[pl.BlockSpec((tm, dw), lambda i, t, e, *_: (t[i], 0)),
                  pl.BlockSpec((1, 1, d, de), lambda i, t, e, *_: (layer, e[i], 0, 0)),
                  pl.BlockSpec((1, 1, d, de), lambda i, t, e, *_: (layer, e[i], 0, 0)),
                  pl.BlockSpec((1, 1, de, d), lambda i, t, e, *_: (layer, e[i], 0, 0))],
        out_specs=pl.BlockSpec((tm, dw), lambda i, t, e, *_: (t[i], 0)),
        scratch_shapes=[pltpu.VMEM((d, de), BF16), pltpu.VMEM((d, de), BF16), pltpu.VMEM((de, d), BF16),
                        pltpu.VMEM((tm, d), F32)],
    )
    return pl.pallas_call(
        _expert_kernel,
        grid_spec=grid_spec,
        out_shape=jax.ShapeDtypeStruct((n_rows, dw), U32),
        compiler_params=_params(("arbitrary",)),
        name="moe_experts",
    )(*items, xs, w1, w3, w2)


def _combine_kernel(dest_ref, dest_next_ref, x_ref, h_ref, w_ref, mod_ref, s1_ref, s3_ref, s2_ref, fg_ref, ys_ref,
                    *outs_and_scratch, final, split_tiles):
    *o_refs, buf, sem = outs_and_scratch
    i = pl.program_id(0)
    tm = x_ref.shape[0]
    slot = i % 2

    def issue(dref, s):
        def body(t, carry):
            for k in range(TOP_K):
                pltpu.make_async_copy(ys_ref.at[pl.ds(dref[k, t], 1), :], buf.at[s, k, pl.ds(t, 1), :],
                                      sem.at[s]).start(priority=k % DMA_PRIORITIES)
            return carry
        lax.fori_loop(0, tm, body, 0)

    @pl.when(i == 0)
    def _():
        issue(dest_ref, slot)

    @pl.when(i + 1 < pl.num_programs(0))
    def _():
        issue(dest_next_ref, 1 - slot)

    h_lo, h_hi = _unpack_rows(h_ref[...])
    hb = jnp.concatenate([h_lo, h_hi], axis=1).astype(BF16)
    a = jnp.dot(hb, s1_ref[...], preferred_element_type=F32)
    b = jnp.dot(hb, s3_ref[...], preferred_element_type=F32)
    acc = jnp.dot((a * jax.nn.sigmoid(a) * b).astype(BF16), s2_ref[...], preferred_element_type=F32)
    for k in range(TOP_K):
        pltpu.make_async_copy(ys_ref.at[pl.ds(0, tm), :], buf.at[slot, k], sem.at[slot]).wait()
    w = w_ref[...]
    half = acc.shape[1] // 2
    acc_lo, acc_hi = acc[:, :half], acc[:, half:]
    for k in range(TOP_K):
        y_lo, y_hi = _unpack_rows(buf[slot, k])
        acc_lo = acc_lo + w[:, k:k + 1] * y_lo
        acc_hi = acc_hi + w[:, k:k + 1] * y_hi
    acc = jnp.concatenate([acc_lo, acc_hi], axis=1)
    m = mod_ref[0]
    y = x_ref[...] + m[5:6] * acc
    if final:
        y = y * lax.rsqrt(jnp.mean(y * y, axis=-1, keepdims=True) + EPS) * fg_ref[...]
    if split_tiles is None:
        o_refs[0][...] = y
    else:
        @pl.when(i < split_tiles)
        def _():
            o_refs[0][...] = y

        @pl.when(i >= split_tiles)
        def _():
            o_refs[1][...] = y


def _combine(x, h, ys, dest, w, mod, s1, s3, s2, final_g, cmap, final, split_rows=None):
    n, d = x.shape
    ds_ = s1.shape[1]
    tm = TM_ROWS
    last = n // tm - 1
    if split_rows is None:
        split_tiles = None
        out_specs = pl.BlockSpec((tm, d), lambda i: (i, 0))
        out_shape = jax.ShapeDtypeStruct((n, d), F32)
    else:
        split_tiles = split_rows // tm
        out_specs = _two_sources(tm, d, split_tiles)
        out_shape = [jax.ShapeDtypeStruct((split_rows, d), F32), jax.ShapeDtypeStruct((n - split_rows, d), F32)]
    return pl.pallas_call(
        functools.partial(_combine_kernel, final=final, split_tiles=split_tiles),
        grid=(n // tm,),
        in_specs=[pl.BlockSpec((TOP_K, tm), lambda i: (0, i), memory_space=pltpu.SMEM),
                  pl.BlockSpec((TOP_K, tm), lambda i: (0, jnp.minimum(i + 1, last)), memory_space=pltpu.SMEM),
                  pl.BlockSpec((tm, d), lambda i: (i, 0)),
                  pl.BlockSpec((tm, d // 2), lambda i: (i, 0)),
                  pl.BlockSpec((tm, TOP_K), lambda i: (i, 0)),
                  pl.BlockSpec((1, N_MOD, d), lambda i: (cmap(i), 0, 0)),
                  pl.BlockSpec((d, ds_), lambda i: (0, 0)),
                  pl.BlockSpec((d, ds_), lambda i: (0, 0)),
                  pl.BlockSpec((ds_, d), lambda i: (0, 0)),
                  pl.BlockSpec((1, d), lambda i: (0, 0)),
                  pl.BlockSpec(memory_space=pl.ANY)],
        out_specs=out_specs,
        out_shape=out_shape,
        scratch_shapes=[pltpu.VMEM((2, TOP_K, tm, d // 2), U32), pltpu.SemaphoreType.DMA((2,))],
        compiler_params=_params(("arbitrary",)),
        name="moe_combine",
    )(dest, dest, x, h, w, mod, s1, s3, s2, final_g.reshape(1, d), ys)


def _moe(x, mod, layer, cmaps, p, final, split_rows=None):
    cmap_r, cmap_c = cmaps
    n, d = x.shape
    rw = p['moe_router_w'][layer].T
    rw_hi = rw.astype(BF16)
    rw_lo = (rw - rw_hi.astype(F32)).astype(BF16)
    h, idx, wts, pos, cnt = _router(x, mod, p['norm2_g'][layer], rw_hi, rw_lo, p['moe_router_b'][layer], cmap_r)
    counts = cnt[:, 0].astype(I32)
    start = jnp.cumsum(counts) - counts
    dest = _dest(idx, pos, start.astype(F32).reshape(-1, 1))
    xs = _dispatch(h, dest)
    items = _work_items(counts, n * TOP_K, TM_EXPERT)
    ys = _experts(xs, items, p['moe_w1'], p['moe_w3'], p['moe_w2'], layer)
    return _combine(x, h, ys, dest, wts.T, mod,
                    p['moe_shared_w1'][layer].astype(BF16), p['moe_shared_w3'][layer].astype(BF16),
                    p['moe_shared_w2'][layer].astype(BF16), p['final_g'], cmap_c, final, split_rows)


def kernel(x_prompt, x_sample, c, state_ssm, cache_k, cache_v, c_ctx, norm1_g, norm2_g, ada_w, ada_b, ssm_lam_re, ssm_lam_im, ssm_log_dt, ssm_b_re, ssm_b_im, ssm_c_re, ssm_c_im, ssm_d, ssm_glu_w, attn_w_qkv, attn_lam_q, attn_lam_k, attn_subln_g, attn_w_o, moe_router_w, moe_router_b, moe_w1, moe_w3, moe_w2, moe_shared_w1, moe_shared_w3, moe_shared_w2, final_g):
    p = dict(norm2_g=norm2_g, moe_router_w=moe_router_w, moe_router_b=moe_router_b,
             moe_w1=moe_w1, moe_w3=moe_w3, moe_w2=moe_w2, moe_shared_w1=moe_shared_w1,
             moe_shared_w3=moe_shared_w3, moe_shared_w2=moe_shared_w2, final_g=final_g)
    b_ctx, l_ctx, d = x_prompt.shape
    b_lat, l_lat, _ = x_sample.shape
    n_ctx, n_lat = b_ctx * l_ctx, b_lat * l_lat
    g = d // SSM_GROUP
    pdim = ssm_lam_re.shape[-1]
    nh = d // V_DIM
    assert 2 * pdim == LANES and CHUNK * SSM_GROUP == 2 * LANES
    assert n_ctx % TM_DENSE == 0 and l_lat % TM_DENSE == 0 and b_lat % 8 == 0 and b_ctx % 8 == 0
    nc_lat, nc_ctx = l_lat // CHUNK, l_ctx // CHUNK
    assert (b_lat * nc_lat) % (b_ctx * nc_ctx) == 0
    cmap = _cond_map(TM_DENSE, n_lat, l_lat)
    cmap_r = _cond_map(TM_ROUTER, n_lat, l_lat)
    cmap_c = _cond_map(TM_ROWS, n_lat, l_lat)

    nc = 1 + b_lat
    nc_pad = -(-nc // 8) * 8
    cond = jnp.concatenate([c_ctx[None, :], c, jnp.zeros((nc_pad - nc, d), F32)], axis=0)
    mod = _ada(cond, ada_w, ada_b)

    x_lat, x_ctx = x_sample.reshape(n_lat, d), x_prompt.reshape(n_ctx, d)

    u = _s5_pack(x_lat, x_ctx, mod[0], norm1_g[0], cmap)
    wts = _s5_weights(ssm_lam_re[0], ssm_lam_im[0], ssm_log_dt[0], ssm_b_re[0], ssm_b_im[0],
                      ssm_c_re[0], ssm_c_im[0])
    h0 = state_ssm[:, 0].transpose(3, 0, 2, 1, 4).reshape(g, b_lat, 2 * LANES)
    y_lat, _ = _s5(u, h0, wts, 0, b_lat, nc_lat)
    y_ctx, fin = _s5(u, jnp.zeros((g, b_ctx, 2 * LANES), F32), wts, b_lat * nc_lat, b_ctx, nc_ctx)
    new_state = fin.reshape(g, b_ctx, 2, 2, pdim).transpose(1, 3, 2, 0, 4)[:, None]
    x = _glu(x_lat, x_ctx, y_lat, y_ctx, mod[0], norm1_g[0], ssm_d[0], ssm_glu_w[0].astype(BF16), cmap)
    x = _moe(x, mod[0], 0, (cmap_r, cmap_c), p, final=False)

    lam_init = 0.8 - 0.6 * math.exp(-0.3 * 1)
    wqkv = attn_w_qkv[0].astype(BF16)
    q_l, k_l, v_l = _qkv(x, mod[1], norm1_g[1], wqkv, cmap, 0, n_lat, rope_l=l_lat)
    q_c, k_c, v_c, kf, vf = _qkv(x, mod[1], norm1_g[1], wqkv, cmap, n_lat, n_ctx)
    new_k = kf.reshape(b_ctx, 1, l_ctx, nh, 2, HEAD_DIM)
    new_v = vf.reshape(b_ctx, 1, l_ctx, nh, V_DIM)
    o_c = _attention(q_c.reshape(b_ctx, l_ctx, d), k_c.reshape(b_ctx, l_ctx, d), v_c.reshape(b_ctx, l_ctx, d),
                     attn_lam_q[0], attn_lam_k[0], attn_subln_g[0], lam_init)
    past = cache_k.shape[2]
    k_all = jnp.concatenate([k_l.reshape(b_lat, l_lat, d), cache_k[:, 0].reshape(b_lat, past, d).astype(BF16)], axis=1)
    v_all = jnp.concatenate([v_l.reshape(b_lat, l_lat, d), cache_v[:, 0].reshape(b_lat, past, d).astype(BF16)], axis=1)
    o_l = _attention(q_l.reshape(b_lat, l_lat, d), k_all, v_all,
                     attn_lam_q[0], attn_lam_k[0], attn_subln_g[0], lam_init)
    x = _oproj(x, o_l.reshape(n_lat, d), o_c.reshape(n_ctx, d), mod[1], attn_w_o[0].astype(BF16), cmap)
    y_lat, y_ctx = _moe(x, mod[1], 1, (cmap_r, cmap_c), p, final=True, split_rows=n_lat)

    return (y_ctx.reshape(b_ctx, l_ctx, d), y_lat.reshape(b_lat, l_lat, d), new_state, new_k, new_v)
```

```python
import functools
import math

import jax
import jax.numpy as jnp
from jax import lax
from jax.experimental import pallas as pl
from jax.experimental.pallas import tpu as pltpu

F32 = jnp.float32
BF16 = jnp.bfloat16
I32 = jnp.int32
U32 = jnp.uint32

EPS = 1e-6
SSM_GROUP = 16
CHUNK = 16
GRID_W = 64
HEAD_DIM = 64
V_DIM = 2 * HEAD_DIM
ROPE_THETA = 10000.0
TOP_K = 8
N_EXPERT_GROUPS = 8
TOPK_GROUPS = 4
ROUTED_SCALE = 2.5
N_MOD = 6

LANES = 128
VMEM_LIMIT = 48 * 1024 * 1024

TM_DENSE = 512
TM_ROUTER = 256
TM_EXPERT = 512
TM_ROWS = 512
TQ_ATTN = 256
DMA_PRIORITIES = 2


def _params(sem):
    return pltpu.CompilerParams(dimension_semantics=sem, vmem_limit_bytes=VMEM_LIMIT)


def _normmod(x, g, sc, sh):
    ms = jnp.mean(x * x, axis=-1, keepdims=True)
    return (x * lax.rsqrt(ms + EPS) * g) * (1.0 + sc) + sh


def _cond_map(tm, n_lat, dec_seq):
    t_lat = n_lat // tm
    per = dec_seq // tm

    def f(i):
        return jnp.where(i < t_lat, 1 + i // per, 0)
    return f


def _ada_kernel(c_ref, w_ref, b_ref, o_ref):
    c = c_ref[...]
    s = (c * jax.nn.sigmoid(c)).astype(BF16)
    o_ref[0] = jnp.dot(s, w_ref[0].astype(BF16), preferred_element_type=F32) + b_ref[0]


def _ada(cond, ada_w, ada_b):
    depth, d, n6 = ada_w.shape
    nc = cond.shape[0]
    tn = 1536
    out = pl.pallas_call(
        _ada_kernel,
        grid=(depth, n6 // tn),
        in_specs=[pl.BlockSpec((nc, d), lambda l, j: (0, 0)),
                  pl.BlockSpec((1, d, tn), lambda l, j: (l, 0, j)),
                  pl.BlockSpec((1, 1, tn), lambda l, j: (l, 0, j))],
        out_specs=pl.BlockSpec((1, nc, tn), lambda l, j: (l, 0, j)),
        out_shape=jax.ShapeDtypeStruct((depth, nc, n6), F32),
        compiler_params=_params(("arbitrary", "arbitrary")),
        name="ada",
    )(cond, ada_w, ada_b.reshape(depth, 1, n6))
    return out.reshape(depth, nc, N_MOD, d)


GROUPS_PER_SLAB = LANES // SSM_GROUP
HALF_CHUNK = CHUNK // 2


def _chunk_perm():
    r = jnp.arange(HALF_CHUNK * LANES)
    t8, g8, c = r // LANES, (r // SSM_GROUP) % GROUPS_PER_SLAB, r % SSM_GROUP
    col = g8 * LANES + t8 * SSM_GROUP + c
    return (col[:, None] == r[None, :]).astype(BF16)


def _two_sources(tm, d, tiles_a):
    last_a = tiles_a - 1
    return [pl.BlockSpec((tm, d), lambda i: (jnp.minimum(i, last_a), 0)),
            pl.BlockSpec((tm, d), lambda i: (jnp.maximum(i - tiles_a, 0), 0))]


def _pick(a_ref, b_ref, tiles_a):
    return jnp.where(pl.program_id(0) < tiles_a, a_ref[...], b_ref[...])


def _pack_kernel(xa_ref, xb_ref, mod_ref, g_ref, p_ref, o_ref, h_s, *, tiles_a):
    m = mod_ref[0]
    h = _normmod(_pick(xa_ref, xb_ref, tiles_a), g_ref[...], m[1:2], m[0:1])
    tm, d = h.shape
    nct = tm // CHUNK
    for j in range(d // LANES):
        h_s[j] = h[:, j * LANES:(j + 1) * LANES]
    slabs = []
    for j in range(d // LANES):
        for hf in range(2):
            slabs.append(jnp.concatenate(
                [h_s[j, pl.ds(hf * HALF_CHUNK + t8, nct, stride=CHUNK), :] for t8 in range(HALF_CHUNK)], axis=1))
    lhs = jnp.concatenate(slabs, axis=0).astype(BF16)
    out = jnp.dot(lhs, p_ref[...], preferred_element_type=F32)
    kw = CHUNK * SSM_GROUP
    for j in range(d // LANES):
        for hf in range(2):
            blk = out[(2 * j + hf) * nct:(2 * j + hf + 1) * nct]
            for g8 in range(GROUPS_PER_SLAB):
                col = (j * GROUPS_PER_SLAB + g8) * kw + hf * LANES
                o_ref[:, col:col + LANES] = blk[:, g8 * LANES:(g8 + 1) * LANES]


def _s5_pack(x_a, x_b, mod, g, cmap):
    d = x_a.shape[1]
    n = x_a.shape[0] + x_b.shape[0]
    tm = TM_DENSE
    tiles_a = x_a.shape[0] // tm
    return pl.pallas_call(
        functools.partial(_pack_kernel, tiles_a=tiles_a),
        grid=(n // tm,),
        in_specs=_two_sources(tm, d, tiles_a) + [
                  pl.BlockSpec((1, N_MOD, d), lambda i: (cmap(i), 0, 0)),
                  pl.BlockSpec((1, d), lambda i: (0, 0)),
                  pl.BlockSpec((HALF_CHUNK * LANES, HALF_CHUNK * LANES), lambda i: (0, 0))],
        out_specs=pl.BlockSpec((tm // CHUNK, CHUNK * d), lambda i: (i, 0)),
        out_shape=jax.ShapeDtypeStruct((n // CHUNK, CHUNK * d), F32),
        scratch_shapes=[pltpu.VMEM((d // LANES, tm, LANES), F32)],
        compiler_params=_params(("arbitrary",)),
        name="s5_pack",
    )(x_a, x_b, mod, g.reshape(1, d), _chunk_perm())


def _s5_weights(lam_re, lam_im, log_dt, b_re, b_im, c_re, c_im):
    hi = lax.Precision.HIGHEST
    t = CHUNK
    w_in, w_out, toep, a16 = [], [], [], []
    for dr in range(2):
        dt = jnp.exp(log_dt[dr])[:, None]
        lr, li = lam_re[dr], lam_im[dr]
        mag = jnp.exp(lr * dt)
        a_re = mag * jnp.cos(li * dt)
        a_im = mag * jnp.sin(li * dt)
        den = lr * lr + li * li
        f_re = ((a_re - 1.0) * lr + a_im * li) / den
        f_im = (a_im * lr - (a_re - 1.0) * li) / den
        bb_re = f_re[..., None] * b_re[dr] - f_im[..., None] * b_im[dr]
        bb_im = f_re[..., None] * b_im[dr] + f_im[..., None] * b_re[dr]
        k = jnp.arange(t + 1, dtype=F32)[:, None, None]
        pm = jnp.exp(k * (lr * dt)[None])
        pk_re = pm * jnp.cos(k * (li * dt)[None])
        pk_im = pm * jnp.sin(k * (li * dt)[None])
        ks = jnp.arange(t)
        sel = (t - 1 - ks) if dr == 0 else ks
        pr, pi = pk_re[sel], pk_im[sel]
        wi_re = pr[..., None] * bb_re[None] - pi[..., None] * bb_im[None]
        wi_im = pr[..., None] * bb_im[None] + pi[..., None] * bb_re[None]
        g_, p_, c_ = bb_re.shape
        w_in.append((wi_re.transpose(1, 0, 3, 2).reshape(g_, t * c_, p_),
                     wi_im.transpose(1, 0, 3, 2).reshape(g_, t * c_, p_)))
        sel = (ks + 1) if dr == 0 else (t - ks)
        pr, pi = pk_re[sel], pk_im[sel]
        ca_re = c_re[dr][None] * pr[:, :, None, :] - c_im[dr][None] * pi[:, :, None, :]
        ca_im = c_re[dr][None] * pi[:, :, None, :] + c_im[dr][None] * pr[:, :, None, :]
        w_out.append((ca_re.transpose(1, 3, 0, 2).reshape(g_, p_, t * c_),
                      (-ca_im).transpose(1, 3, 0, 2).reshape(g_, p_, t * c_)))
        pr, pi = pk_re[:t], pk_im[:t]
        cd_re = c_re[dr][None] * pr[:, :, None, :] - c_im[dr][None] * pi[:, :, None, :]
        cd_im = c_re[dr][None] * pi[:, :, None, :] + c_im[dr][None] * pr[:, :, None, :]
        kk = jnp.einsum('kgop,gpi->kgoi', jnp.concatenate([cd_re, -cd_im], axis=3),
                        jnp.concatenate([bb_re, bb_im], axis=1), precision=hi)
        s_i = ks[:, None]
        t_i = ks[None, :]
        delta = (t_i - s_i) if dr == 0 else (s_i - t_i)
        kd = jnp.where((delta >= 0)[:, :, None, None, None], kk[jnp.clip(delta, 0, t - 1)], 0.0)
        toep.append(kd.transpose(2, 0, 4, 1, 3).reshape(g_, t * c_, t * c_))
        a16.append((pk_re[t], pk_im[t]))
    w_in_all = jnp.concatenate([w_in[0][0], w_in[1][0], w_in[0][1], w_in[1][1]], axis=2)
    w_out_all = jnp.concatenate([w_out[0][0], w_out[1][0], w_out[0][1], w_out[1][1]], axis=1)
    toep_all = toep[0] + toep[1]
    a16_all = jnp.stack([jnp.concatenate([a16[0][0], a16[1][0]], axis=1),
                         jnp.concatenate([a16[0][1], a16[1][1]], axis=1)], axis=1)
    return w_in_all.astype(BF16), toep_all.astype(BF16), w_out_all.astype(BF16), a16_all


def _s5_kernel(u_hbm, win_ref, wtp_ref, wout_ref, a16_ref, h0_ref, y_hbm, fin_ref,
               ubuf, ybuf, v_s, sf_s, sb_s, sem_in, sem_out, *, nc, nb, row0):
    g = pl.program_id(0)
    ng = pl.num_programs(0)
    kw = ubuf.shape[3]
    slot = g % 2

    def in_copy(grp, s, b):
        return pltpu.make_async_copy(u_hbm.at[pl.ds(row0 + b * nc, nc), pl.ds(grp * kw, kw)],
                                     ubuf.at[s, :, b, :], sem_in.at[s])

    def out_copy(grp, s, b):
        return pltpu.make_async_copy(ybuf.at[s, :, b, :],
                                     y_hbm.at[pl.ds(b * nc, nc), pl.ds(grp * kw, kw)], sem_out.at[s])

    @pl.when(g == 0)
    def _():
        for b in range(nb):
            in_copy(g, slot, b).start()

    @pl.when(g + 1 < ng)
    def _():
        for b in range(nb):
            in_copy(g + 1, 1 - slot, b).start()

    for b in range(nb):
        in_copy(g, slot, b).wait()
    u = ubuf[slot].reshape(nc * nb, kw).astype(BF16)
    v_s[...] = jnp.dot(u, win_ref[0], preferred_element_type=F32)
    a = a16_ref[0]
    ar, ai = a[0:1], a[1:2]
    fwd = (lax.broadcasted_iota(I32, (nb, 2 * LANES), 1) % LANES) < (LANES // 2)

    def step(i, carry):
        sr, si = carry
        st = jnp.concatenate([sr, si], axis=1)
        rf = pl.multiple_of(i * nb, nb)
        rb = pl.multiple_of((nc - 1 - i) * nb, nb)
        sf_s[pl.ds(rf, nb), :] = st
        sb_s[pl.ds(rb, nb), :] = st
        v = jnp.where(fwd, v_s[pl.ds(rf, nb), :], v_s[pl.ds(rb, nb), :])
        return ar * sr - ai * si + v[:, :LANES], ar * si + ai * sr + v[:, LANES:]

    h0 = h0_ref[0]
    sr, si = lax.fori_loop(0, nc, step, (h0[:, :LANES], h0[:, LANES:]))
    fin_ref[0] = jnp.concatenate([sr, si], axis=1)
    fwd_all = (lax.broadcasted_iota(I32, (nc * nb, 2 * LANES), 1) % LANES) < (LANES // 2)
    hs = jnp.where(fwd_all, sf_s[...], sb_s[...]).astype(BF16)
    y = (jnp.dot(u, wtp_ref[0], preferred_element_type=F32)
         + jnp.dot(hs, wout_ref[0], preferred_element_type=F32))

    @pl.when(g >= 2)
    def _():
        for b in range(nb):
            out_copy(g - 2, slot, b).wait()

    ybuf[slot] = y.reshape(nc, nb, kw)
    for b in range(nb):
        out_copy(g, slot, b).start()

    @pl.when(g == ng - 1)
    def _():
        if ng > 1:
            for b in range(nb):
                out_copy(g - 1, 1 - slot, b).wait()
        for b in range(nb):
            out_copy(g, slot, b).wait()


def _s5(u, h0, wts, row0, nb, nc):
    w_in, w_tp, w_out, a16 = wts
    r = nb * nc
    kw = CHUNK * SSM_GROUP
    g = u.shape[1] // kw
    return pl.pallas_call(
        functools.partial(_s5_kernel, nc=nc, nb=nb, row0=row0),
        grid=(g,),
        in_specs=[pl.BlockSpec(memory_space=pl.ANY),
                  pl.BlockSpec((1, kw, kw), lambda j: (j, 0, 0)),
                  pl.BlockSpec((1, kw, kw), lambda j: (j, 0, 0)),
                  pl.BlockSpec((1, kw, kw), lambda j: (j, 0, 0)),
                  pl.BlockSpec((1, 2, LANES), lambda j: (j, 0, 0)),
                  pl.BlockSpec((1, nb, kw), lambda j: (j, 0, 0))],
        out_specs=[pl.BlockSpec(memory_space=pl.ANY),
                   pl.BlockSpec((1, nb, kw), lambda j: (j, 0, 0))],
        out_shape=[jax.ShapeDtypeStruct((r, u.shape[1]), F32),
                   jax.ShapeDtypeStruct((g, nb, kw), F32)],
        scratch_shapes=[pltpu.VMEM((2, nc, nb, kw), F32), pltpu.VMEM((2, nc, nb, kw), F32),
                        pltpu.VMEM((r, kw), F32), pltpu.VMEM((r, kw), F32), pltpu.VMEM((r, kw), F32),
                        pltpu.SemaphoreType.DMA((2,)), pltpu.SemaphoreType.DMA((2,))],
        compiler_params=_params(("arbitrary",)),
        name="s5_scan",
    )(u, w_in, w_tp, w_out, a16, h0)


def _glu_kernel(xa_ref, xb_ref, ya_ref, yb_ref, mod_ref, g_ref, d_ref, pt_ref, w_ref, o_ref, y_s, *, tiles_a):
    tm, dd = xa_ref.shape
    nct = tm // CHUNK
    kw = CHUNK * SSM_GROUP
    from_a = pl.program_id(0) < tiles_a
    slabs = []
    for j in range(dd // LANES):
        for hf in range(2):
            cols = [(j * GROUPS_PER_SLAB + g8) * kw + hf * LANES for g8 in range(GROUPS_PER_SLAB)]
            slabs.append(jnp.concatenate(
                [jnp.where(from_a, ya_ref[:, c0:c0 + LANES], yb_ref[:, c0:c0 + LANES]) for c0 in cols], axis=1))
    lhs = jnp.concatenate(slabs, axis=0)
    hi = lhs.astype(BF16)
    lo = (lhs - hi.astype(F32)).astype(BF16)
    pt = pt_ref[...]
    nat = jnp.dot(hi, pt, preferred_element_type=F32) + jnp.dot(lo, pt, preferred_element_type=F32)
    for j in range(dd // LANES):
        for hf in range(2):
            blk = nat[(2 * j + hf) * nct:(2 * j + hf + 1) * nct]
            for t8 in range(HALF_CHUNK):
                y_s[j, pl.ds(hf * HALF_CHUNK + t8, nct, stride=CHUNK), :] = blk[:, t8 * LANES:(t8 + 1) * LANES]
    m = mod_ref[0]
    x = _pick(xa_ref, xb_ref, tiles_a)
    h = _normmod(x, g_ref[...], m[1:2], m[0:1])
    y = d_ref[...] * h + jnp.concatenate([y_s[j] for j in range(dd // LANES)], axis=1)
    gl = jax.nn.gelu(y).astype(BF16)
    z = jnp.dot(gl, w_ref[...], preferred_element_type=F32)
    o_ref[...] = x + m[2:3] * (z[:, :dd] * jax.nn.sigmoid(z[:, dd:]))


def _glu(x_a, x_b, y_a, y_b, mod, g, dskip, w, cmap):
    d = x_a.shape[1]
    n = x_a.shape[0] + x_b.shape[0]
    tm = TM_DENSE
    nct = tm // CHUNK
    tiles_a = y_a.shape[0] // nct
    assert tiles_a == x_a.shape[0] // tm
    last_a = tiles_a - 1
    return pl.pallas_call(
        functools.partial(_glu_kernel, tiles_a=tiles_a),
        grid=(n // tm,),
        in_specs=_two_sources(tm, d, tiles_a) + [
                  pl.BlockSpec((nct, CHUNK * d), lambda i: (jnp.minimum(i, last_a), 0)),
                  pl.BlockSpec((nct, CHUNK * d), lambda i: (jnp.maximum(i - tiles_a, 0), 0)),
                  pl.BlockSpec((1, N_MOD, d), lambda i: (cmap(i), 0, 0)),
                  pl.BlockSpec((1, d), lambda i: (0, 0)),
                  pl.BlockSpec((1, d), lambda i: (0, 0)),
                  pl.BlockSpec((HALF_CHUNK * LANES, HALF_CHUNK * LANES), lambda i: (0, 0)),
                  pl.BlockSpec((d, 2 * d), lambda i: (0, 0))],
        out_specs=pl.BlockSpec((tm, d), lambda i: (i, 0)),
        out_shape=jax.ShapeDtypeStruct((n, d), F32),
        scratch_shapes=[pltpu.VMEM((d // LANES, tm, LANES), F32)],
        compiler_params=_params(("arbitrary",)),
        name="s5_glu",
    )(x_a, x_b, y_a, y_b, mod, g.reshape(1, d), dskip.reshape(1, d), _chunk_perm().T, w)


def _rope_tables(l):
    r = HEAD_DIM // 2
    half = r // 2
    freq = ROPE_THETA ** (-jnp.arange(half, dtype=F32) / half)
    pos = jnp.arange(l)
    row = (pos // GRID_W).astype(F32)
    col = (pos % GRID_W).astype(F32)
    d = jnp.arange(LANES) % HEAD_DIM
    p = jnp.where((d < r)[None, :], row[:, None], col[:, None])
    ang = p * freq[d % half][None, :]
    sign = jnp.where((d % r) < half, -1.0, 1.0)[None, :]
    return jnp.cos(ang), jnp.sin(ang) * sign


def _qkv_kernel(*refs, rope):
    if rope:
        x_ref, mod_ref, g_ref, w_ref, cos_ref, sin_ref, q_ref, k_ref, v_ref = refs
    else:
        x_ref, mod_ref, g_ref, w_ref, q_ref, k_ref, v_ref, kf_ref, vf_ref = refs
    m = mod_ref[0]
    h = _normmod(x_ref[...], g_ref[...], m[1:2], m[0:1]).astype(BF16)
    qkv = jnp.dot(h, w_ref[...], preferred_element_type=F32)
    d = x_ref.shape[1]
    q = qkv[:, :d] * (HEAD_DIM ** -0.5 * math.log2(math.e))
    k = qkv[:, d:2 * d]
    v = qkv[:, 2 * d:]
    v_ref[...] = v.astype(BF16)
    if not rope:
        q_ref[...] = q.astype(BF16)
        k_ref[...] = k.astype(BF16)
        kf_ref[...] = k
        vf_ref[...] = v
        return
    cos = cos_ref[...]
    sin = sin_ref[...]
    half = HEAD_DIM // 4
    lane = lax.broadcasted_iota(I32, cos.shape, 1)
    up_sel = (lane % (2 * half)) < half
    for src, dst in ((q, q_ref), (k, k_ref)):
        for j in range(d // LANES):
            blk = src[:, j * LANES:(j + 1) * LANES]
            up = pltpu.roll(blk, LANES - half, 1)
            dn = pltpu.roll(blk, half, 1)
            dst[:, j * LANES:(j + 1) * LANES] = (blk * cos + jnp.where(up_sel, up, dn) * sin).astype(BF16)


def _qkv(x, mod, g, w, cmap, row0, nrows, rope_l=None):
    n, d = x.shape
    tm = TM_DENSE
    t0 = row0 // tm
    rope = rope_l is not None
    in_specs = [pl.BlockSpec((tm, d), lambda i: (i + t0, 0)),
                pl.BlockSpec((1, N_MOD, d), lambda i: (cmap(i + t0), 0, 0)),
                pl.BlockSpec((1, d), lambda i: (0, 0)),
                pl.BlockSpec((d, 3 * d), lambda i: (0, 0))]
    args = [x, mod, g.reshape(1, d), w]
    row_spec = pl.BlockSpec((tm, d), lambda i: (i, 0))
    out_specs = [row_spec] * 3
    out_shape = [jax.ShapeDtypeStruct((nrows, d), BF16)] * 3
    if rope:
        per = rope_l // tm
        cos, sin = _rope_tables(rope_l)
        in_specs += [pl.BlockSpec((tm, LANES), lambda i: (i % per, 0))] * 2
        args += [cos, sin]
    else:
        out_specs += [row_spec] * 2
        out_shape += [jax.ShapeDtypeStruct((nrows, d), F32)] * 2
    return pl.pallas_call(
        functools.partial(_qkv_kernel, rope=rope),
        grid=(nrows // tm,),
        in_specs=in_specs,
        out_specs=out_specs,
        out_shape=out_shape,
        compiler_params=_params(("arbitrary",)),
        name="qkv_rope" if rope else "qkv",
    )(*args)


def _attn_kernel(q_ref, kt_ref, v_ref, lq_ref, lk_ref, g_ref, o_ref, *, lam_init):
    q = q_ref[0]
    tq = q.shape[0]
    lane = lax.broadcasted_iota(I32, q.shape, 1)
    zero = jnp.zeros_like(q)
    qq = jnp.concatenate([jnp.where(lane < HEAD_DIM, q, zero), jnp.where(lane >= HEAD_DIM, q, zero)], axis=0)
    s = jnp.dot(qq, kt_ref[0, 0], preferred_element_type=F32)
    p = jnp.exp2(s - jnp.max(s, axis=1, keepdims=True))
    l = jnp.sum(p, axis=1, keepdims=True)
    lqk = lq_ref[...] * lk_ref[...]
    lam = (jnp.exp(jnp.sum(lqk[0:1], axis=1, keepdims=True))
           - jnp.exp(jnp.sum(lqk[1:2], axis=1, keepdims=True)) + lam_init)
    w = p[:tq] * (1.0 / l[:tq]) - p[tq:] * (lam / l[tq:])
    o = jnp.dot(w.astype(BF16), v_ref[0], preferred_element_type=F32)
    o = o * lax.rsqrt(jnp.mean(o * o, axis=1, keepdims=True) + EPS) * g_ref[...]
    o_ref[0] = (o * (1.0 - lam_init)).astype(o_ref.dtype)


def _attention(q, k, v, lam_q, lam_k, subln_g, lam_init):
    nb, l, d = q.shape
    lk = k.shape[1]
    nh = d // V_DIM
    kt = k.reshape(nb, lk, nh, V_DIM).transpose(0, 2, 3, 1)
    tq = min(TQ_ATTN, l)
    return pl.pallas_call(
        functools.partial(_attn_kernel, lam_init=lam_init),
        grid=(nb, nh, l // tq),
        in_specs=[pl.BlockSpec((1, tq, V_DIM), lambda b, h, i: (b, i, h)),
                  pl.BlockSpec((1, 1, V_DIM, lk), lambda b, h, i: (b, h, 0, 0)),
                  pl.BlockSpec((1, lk, V_DIM), lambda b, h, i: (b, 0, h)),
                  pl.BlockSpec((2, HEAD_DIM), lambda b, h, i: (0, 0)),
                  pl.BlockSpec((2, HEAD_DIM), lambda b, h, i: (0, 0)),
                  pl.BlockSpec((1, V_DIM), lambda b, h, i: (0, 0))],
        out_specs=pl.BlockSpec((1, tq, V_DIM), lambda b, h, i: (b, i, h)),
        out_shape=jax.ShapeDtypeStruct((nb, l, d), BF16),
        compiler_params=_params(("arbitrary", "arbitrary", "arbitrary")),
        name="diff_attn",
    )(q, kt, v, lam_q, lam_k, subln_g.reshape(1, V_DIM))


def _oproj_kernel(x_ref, oa_ref, ob_ref, mod_ref, w_ref, y_ref, *, tiles_a):
    m = mod_ref[0]
    o = _pick(oa_ref, ob_ref, tiles_a)
    y_ref[...] = x_ref[...] + m[2:3] * jnp.dot(o, w_ref[...], preferred_element_type=F32)


def _oproj(x, o_a, o_b, mod, w, cmap):
    n, d = x.shape
    tm = TM_DENSE
    tiles_a = o_a.shape[0] // tm
    return pl.pallas_call(
        functools.partial(_oproj_kernel, tiles_a=tiles_a),
        grid=(n // tm,),
        in_specs=[pl.BlockSpec((tm, d), lambda i: (i, 0))] + _two_sources(tm, d, tiles_a) + [
                  pl.BlockSpec((1, N_MOD, d), lambda i: (cmap(i), 0, 0)),
                  pl.BlockSpec((d, d), lambda i: (0, 0))],
        out_specs=pl.BlockSpec((tm, d), lambda i: (i, 0)),
        out_shape=jax.ShapeDtypeStruct((n, d), F32),
        compiler_params=_params(("arbitrary",)),
        name="attn_out",
    )(x, o_a, o_b, mod, w)


def _pack_rows(h):
    half = h.shape[1] // 2
    bits = lax.bitcast_convert_type(h.astype(BF16).astype(F32), U32)
    return bits[:, half:] | (bits[:, :half] >> 16)


def _unpack_rows(w):
    lo = lax.bitcast_convert_type(w << 16, F32)
    hi = lax.bitcast_convert_type(w & jnp.uint32(0xFFFF0000), F32)
    return lo, hi


def _stack_rows(rows, width):
    ri = lax.broadcasted_iota(I32, (len(rows), width), 0)
    out = jnp.zeros((len(rows), width), rows[0].dtype)
    for j, r in enumerate(rows):
        out = jnp.where(ri == j, r, out)
    return out


def _router_kernel(x_ref, mod_ref, g_ref, wh_ref, wl_ref, b_ref, h_ref, idx_ref, w_ref, pos_ref, cnt_ref, cnt_s):
    step = pl.program_id(0)

    @pl.when(step == 0)
    def _():
        cnt_s[...] = jnp.zeros_like(cnt_s)

    m = mod_ref[0]
    h = _normmod(x_ref[...], g_ref[...], m[4:5], m[3:4])
    h_ref[...] = _pack_rows(h)
    tr = h.shape[0]
    ne = wh_ref.shape[0]
    per = ne // N_EXPERT_GROUPS
    hh = h.astype(BF16)
    hl = (h - hh.astype(F32)).astype(BF16)
    nt = (((1,), (1,)), ((), ()))
    wh = wh_ref[...]
    logits = (lax.dot_general(wh, hh, nt, preferred_element_type=F32)
              + lax.dot_general(wl_ref[...], hh, nt, preferred_element_type=F32)
              + lax.dot_general(wh, hl, nt, preferred_element_type=F32))
    scores = jax.nn.sigmoid(logits)
    biased = scores + b_ref[...]
    neg = -jnp.inf
    ri = lax.broadcasted_iota(I32, (per, tr), 0).astype(F32)
    blocks, gscores = [], []
    for j in range(N_EXPERT_GROUPS):
        blk = biased[j * per:(j + 1) * per, :]
        m1 = jnp.max(blk, axis=0, keepdims=True)
        f1 = jnp.min(jnp.where(blk == m1, ri, float(per)), axis=0, keepdims=True)
        m2 = jnp.max(jnp.where(ri == f1, neg, blk), axis=0, keepdims=True)
        blocks.append(blk)
        gscores.append(m1 + m2)
    gs = _stack_rows(gscores, tr)
    gi = lax.broadcasted_iota(I32, gs.shape, 0).astype(F32)
    gsel = jnp.zeros(gs.shape, F32)
    for _ in range(TOPK_GROUPS):
        mm = jnp.max(gs, axis=0, keepdims=True)
        pick = gi == jnp.min(jnp.where(gs == mm, gi, float(N_EXPERT_GROUPS)), axis=0, keepdims=True)
        gsel = jnp.where(pick, 1.0, gsel)
        gs = jnp.where(pick, neg, gs)
    masked = jnp.concatenate(
        [jnp.where(gsel[j:j + 1, :] > 0.0, blocks[j], neg) for j in range(N_EXPERT_GROUPS)], axis=0)
    ei = lax.broadcasted_iota(I32, masked.shape, 0).astype(F32)
    picks, idxs, ws = [], [], []
    for _ in range(TOP_K):
        mm = jnp.max(masked, axis=0, keepdims=True)
        ff = jnp.min(jnp.where(masked == mm, ei, float(ne)), axis=0, keepdims=True)
        pick = ei == ff
        picks.append(pick)
        idxs.append(ff)
        ws.append(jnp.sum(jnp.where(pick, scores, 0.0), axis=0, keepdims=True))
        masked = jnp.where(pick, neg, masked)
    wsum = ws[0]
    for wk in ws[1:]:
        wsum = wsum + wk
    w_ref[...] = _stack_rows([wk / wsum * ROUTED_SCALE for wk in ws], tr)
    idx_ref[...] = _stack_rows(idxs, tr).astype(I32)
    sel = jnp.zeros(masked.shape, F32)
    for pick in picks:
        sel = sel + jnp.where(pick, 1.0, 0.0)
    earlier = (lax.broadcasted_iota(I32, (tr, tr), 0) < lax.broadcasted_iota(I32, (tr, tr), 1))
    prefix = jnp.dot(sel.astype(BF16), jnp.where(earlier, 1.0, 0.0).astype(BF16), preferred_element_type=F32)
    base = prefix + cnt_s[...]
    pos_ref[...] = _stack_rows(
        [jnp.sum(jnp.where(pick, base, 0.0), axis=0, keepdims=True) for pick in picks], tr).astype(I32)
    cnt_s[...] = cnt_s[...] + jnp.sum(sel, axis=1, keepdims=True)
    cnt_ref[...] = cnt_s[...]


def _router(x, mod, g, w_t_hi, w_t_lo, bias, cmap):
    n, d = x.shape
    ne = w_t_hi.shape[0]
    tr = TM_ROUTER
    tok_spec = pl.BlockSpec((TOP_K, tr), lambda i: (0, i))
    return pl.pallas_call(
        _router_kernel,
        grid=(n // tr,),
        in_specs=[pl.BlockSpec((tr, d), lambda i: (i, 0)),
                  pl.BlockSpec((1, N_MOD, d), lambda i: (cmap(i), 0, 0)),
                  pl.BlockSpec((1, d), lambda i: (0, 0)),
                  pl.BlockSpec((ne, d), lambda i: (0, 0)),
                  pl.BlockSpec((ne, d), lambda i: (0, 0)),
                  pl.BlockSpec((ne, 1), lambda i: (0, 0))],
        out_specs=[pl.BlockSpec((tr, d // 2), lambda i: (i, 0)), tok_spec, tok_spec, tok_spec,
                   pl.BlockSpec((ne, 1), lambda i: (0, 0))],
        out_shape=[jax.ShapeDtypeStruct((n, d // 2), U32),
                   jax.ShapeDtypeStruct((TOP_K, n), I32),
                   jax.ShapeDtypeStruct((TOP_K, n), F32),
                   jax.ShapeDtypeStruct((TOP_K, n), I32),
                   jax.ShapeDtypeStruct((ne, 1), F32)],
        scratch_shapes=[pltpu.VMEM((ne, 1), F32)],
        compiler_params=_params(("arbitrary",)),
        name="moe_router",
    )(x, mod, g.reshape(1, d), w_t_hi, w_t_lo, bias.reshape(ne, 1))


def _dest_kernel(idx_ref, pos_ref, start_ref, o_ref):
    idx = idx_ref[...]
    ei = lax.broadcasted_iota(I32, (start_ref.shape[0], idx.shape[1]), 0)
    start = start_ref[...]
    rows = [jnp.sum(jnp.where(ei == idx[k:k + 1], start, 0.0), axis=0, keepdims=True) for k in range(TOP_K)]
    o_ref[...] = _stack_rows(rows, idx.shape[1]).astype(I32) + pos_ref[...]


def _dest(idx, pos, start):
    n = idx.shape[1]
    ne = start.shape[0]
    tr = 512
    tok_spec = pl.BlockSpec((TOP_K, tr), lambda i: (0, i))
    return pl.pallas_call(
        _dest_kernel,
        grid=(n // tr,),
        in_specs=[tok_spec, tok_spec, pl.BlockSpec((ne, 1), lambda i: (0, 0))],
        out_specs=tok_spec,
        out_shape=jax.ShapeDtypeStruct((TOP_K, n), I32),
        compiler_params=_params(("arbitrary",)),
        name="moe_dest",
    )(idx, pos, start)


def _dispatch_kernel(dest_ref, h_ref, xs_ref, sem):
    tm = h_ref.shape[0]

    def issue(t, carry):
        for k in range(TOP_K):
            pltpu.make_async_copy(h_ref.at[pl.ds(t, 1), :], xs_ref.at[pl.ds(dest_ref[k, t], 1), :],
                                  sem).start(priority=k % DMA_PRIORITIES)
        return carry

    lax.fori_loop(0, tm, issue, 0)
    for _ in range(TOP_K):
        pltpu.make_async_copy(h_ref, xs_ref.at[pl.ds(0, tm), :], sem).wait()


def _dispatch(h, dest):
    n, d = h.shape
    tm = TM_ROWS
    return pl.pallas_call(
        _dispatch_kernel,
        grid=(n // tm,),
        in_specs=[pl.BlockSpec((TOP_K, tm), lambda i: (0, i), memory_space=pltpu.SMEM),
                  pl.BlockSpec((tm, d), lambda i: (i, 0))],
        out_specs=pl.BlockSpec(memory_space=pl.ANY),
        out_shape=jax.ShapeDtypeStruct((n * TOP_K, d), U32),
        scratch_shapes=[pltpu.SemaphoreType.DMA(())],
        compiler_params=_params(("arbitrary",)),
        name="moe_dispatch",
    )(dest, h)


def _expert_kernel(tile_ref, exp_ref, lo_ref, hi_ref, newe_ref, newt_ref, last_ref,
                   xs_ref, w1_ref, w3_ref, w2_ref, ys_ref, w1_s, w3_s, w2_s, acc_s):
    i = pl.program_id(0)

    @pl.when(newe_ref[i] == 1)
    def _():
        w1_s[...] = w1_ref[0, 0].astype(BF16)
        w3_s[...] = w3_ref[0, 0].astype(BF16)
        w2_s[...] = w2_ref[0, 0].astype(BF16)

    tm = xs_ref.shape[0]
    lo, hi = lo_ref[i], hi_ref[i]

    @pl.when(hi > lo)
    def _():
        x_lo, x_hi = _unpack_rows(xs_ref[...])
        x = jnp.concatenate([x_lo, x_hi], axis=1).astype(BF16)
        a = jnp.dot(x, w1_s[...], preferred_element_type=F32)
        b = jnp.dot(x, w3_s[...], preferred_element_type=F32)
        row = tile_ref[i] * tm + lax.broadcasted_iota(I32, (tm, 1), 0)
        hb = jnp.where((row >= lo) & (row < hi), a * jax.nn.sigmoid(a) * b, 0.0).astype(BF16)
        y = jnp.dot(hb, w2_s[...], preferred_element_type=F32)
        first, last = newt_ref[i] == 1, last_ref[i] == 1

        @pl.when(first & last)
        def _():
            ys_ref[...] = _pack_rows(y)

        @pl.when(first & jnp.logical_not(last))
        def _():
            acc_s[...] = y

        @pl.when(jnp.logical_not(first))
        def _():
            acc_s[...] = acc_s[...] + y

    @pl.when((last_ref[i] == 1) & jnp.logical_not((newt_ref[i] == 1) & (hi > lo)))
    def _():
        ys_ref[...] = _pack_rows(acc_s[...])


def _work_items(counts, n_rows, tm):
    ne = counts.shape[0]
    n_tiles = n_rows // tm
    n_items = n_tiles + ne - 1
    end = jnp.cumsum(counts)
    start = end - counts
    first_t = start // tm
    n_e = jnp.where(counts > 0, (end - 1) // tm - first_t + 1, 0)
    item_end = jnp.cumsum(n_e)
    item_off = item_end - n_e
    it = jnp.arange(n_items, dtype=I32)
    total = item_end[-1]
    valid = it < total
    last_item = jnp.minimum(it, total - 1)
    e = jnp.minimum(jnp.sum((item_end[None, :] <= last_item[:, None]).astype(I32), axis=1), ne - 1).astype(I32)
    tile = jnp.where(valid, first_t[e] + it - item_off[e], n_tiles - 1).astype(I32)
    lo = jnp.where(valid, jnp.maximum(start[e], tile * tm), 0).astype(I32)
    hi = jnp.where(valid, jnp.minimum(end[e], (tile + 1) * tm), 0).astype(I32)
    prev_e = jnp.concatenate([jnp.full((1,), -1, I32), e[:-1]])
    prev_t = jnp.concatenate([jnp.full((1,), -1, I32), tile[:-1]])
    next_t = jnp.concatenate([tile[1:], jnp.full((1,), -1, I32)])
    return (tile, e, lo, hi, (e != prev_e).astype(I32), (tile != prev_t).astype(I32),
            (tile != next_t).astype(I32))


def _experts(xs, items, w1, w3, w2, layer):
    n_rows, dw = xs.shape
    d, de = w1.shape[2], w1.shape[3]
    tm = TM_EXPERT
    n_items = items[0].shape[0]
    grid_spec = pltpu.PrefetchScalarGridSpec(
        num_scalar_prefetch=len(items),
        grid=(n_items,),
        in_specs=[pl.BlockSpec((tm, dw), lambda i, t, e, *_: (t[i], 0)),
                  pl.BlockSpec((1, 1, d, de), lambda i, t, e, *_: (layer, e[i], 0, 0)),
                  pl.BlockSpec((1, 1, d, de), lambda i, t, e, *_: (layer, e[i], 0, 0)),
                  pl.BlockSpec((1, 1, de, d), lambda i, t, e, *_: (layer, e[i], 0, 0))],
        out_specs=pl.BlockSpec((tm, dw), lambda i, t, e, *_: (t[i], 0)),
        scratch_shapes=[pltpu.VMEM((d, de), BF16), pltpu.VMEM((d, de), BF16), pltpu.VMEM((de, d), BF16),
                        pltpu.VMEM((tm, d), F32)],
    )
    return pl.pallas_call(
        _expert_kernel,
        grid_spec=grid_spec,
        out_shape=jax.ShapeDtypeStruct((n_rows, dw), U32),
        compiler_params=_params(("arbitrary",)),
        name="moe_experts",
    )(*items, xs, w1, w3, w2)


def _combine_kernel(dest_ref, dest_next_ref, x_ref, h_ref, w_ref, mod_ref, s1_ref, s3_ref, s2_ref, fg_ref, ys_ref,
                    *outs_and_scratch, final, split_tiles):
    *o_refs, buf, sem = outs_and_scratch
    i = pl.program_id(0)
    tm = x_ref.shape[0]
    slot = i % 2

    def issue(dref, s):
        def body(t, carry):
            for k in range(TOP_K):
                pltpu.make_async_copy(ys_ref.at[pl.ds(dref[k, t], 1), :], buf.at[s, k, pl.ds(t, 1), :],
                                      sem.at[s]).start(priority=k % DMA_PRIORITIES)
            return carry
        lax.fori_loop(0, tm, body, 0)

    @pl.when(i == 0)
    def _():
        issue(dest_ref, slot)

    @pl.when(i + 1 < pl.num_programs(0))
    def _():
        issue(dest_next_ref, 1 - slot)

    h_lo, h_hi = _unpack_rows(h_ref[...])
    hb = jnp.concatenate([h_lo, h_hi], axis=1).astype(BF16)
    a = jnp.dot(hb, s1_ref[...], preferred_element_type=F32)
    b = jnp.dot(hb, s3_ref[...], preferred_element_type=F32)
    acc = jnp.dot((a * jax.nn.sigmoid(a) * b).astype(BF16), s2_ref[...], preferred_element_type=F32)
    for k in range(TOP_K):
        pltpu.make_async_copy(ys_ref.at[pl.ds(0, tm), :], buf.at[slot, k], sem.at[slot]).wait()
    w = w_ref[...]
    half = acc.shape[1] // 2
    acc_lo, acc_hi = acc[:, :half], acc[:, half:]
    for k in range(TOP_K):
        y_lo, y_hi = _unpack_rows(buf[slot, k])
        acc_lo = acc_lo + w[:, k:k + 1] * y_lo
        acc_hi = acc_hi + w[:, k:k + 1] * y_hi
    acc = jnp.concatenate([acc_lo, acc_hi], axis=1)
    m = mod_ref[0]
    y = x_ref[...] + m[5:6] * acc
    if final:
        y = y * lax.rsqrt(jnp.mean(y * y, axis=-1, keepdims=True) + EPS) * fg_ref[...]
    if split_tiles is None:
        o_refs[0][...] = y
    else:
        @pl.when(i < split_tiles)
        def _():
            o_refs[0][...] = y

        @pl.when(i >= split_tiles)
        def _():
            o_refs[1][...] = y


def _combine(x, h, ys, dest, w, mod, s1, s3, s2, final_g, cmap, final, split_rows=None):
    n, d = x.shape
    ds_ = s1.shape[1]
    tm = TM_ROWS
    last = n // tm - 1
    if split_rows is None:
        split_tiles = None
        out_specs = pl.BlockSpec((tm, d), lambda i: (i, 0))
        out_shape = jax.ShapeDtypeStruct((n, d), F32)
    else:
        split_tiles = split_rows // tm
        out_specs = _two_sources(tm, d, split_tiles)
        out_shape = [jax.ShapeDtypeStruct((split_rows, d), F32), jax.ShapeDtypeStruct((n - split_rows, d), F32)]
    return pl.pallas_call(
        functools.partial(_combine_kernel, final=final, split_tiles=split_tiles),
        grid=(n // tm,),
        in_specs=[pl.BlockSpec((TOP_K, tm), lambda i: (0, i), memory_space=pltpu.SMEM),
                  pl.BlockSpec((TOP_K, tm), lambda i: (0, jnp.minimum(i + 1, last)), memory_space=pltpu.SMEM),
                  pl.BlockSpec((tm, d), lambda i: (i, 0)),
                  pl.BlockSpec((tm, d // 2), lambda i: (i, 0)),
                  pl.BlockSpec((tm, TOP_K), lambda i: (i, 0)),
                  pl.BlockSpec((1, N_MOD, d), lambda i: (cmap(i), 0, 0)),
                  pl.BlockSpec((d, ds_), lambda i: (0, 0)),
                  pl.BlockSpec((d, ds_), lambda i: (0, 0)),
                  pl.BlockSpec((ds_, d), lambda i: (0, 0)),
                  pl.BlockSpec((1, d), lambda i: (0, 0)),
                  pl.BlockSpec(memory_space=pl.ANY)],
        out_specs=out_specs,
        out_shape=out_shape,
        scratch_shapes=[pltpu.VMEM((2, TOP_K, tm, d // 2), U32), pltpu.SemaphoreType.DMA((2,))],
        compiler_params=_params(("arbitrary",)),
        name="moe_combine",
    )(dest, dest, x, h, w, mod, s1, s3, s2, final_g.reshape(1, d), ys)


def _moe(x, mod, layer, cmaps, p, final, split_rows=None):
    cmap_r, cmap_c = cmaps
    n, d = x.shape
    rw = p['moe_router_w'][layer].T
    rw_hi = rw.astype(BF16)
    rw_lo = (rw - rw_hi.astype(F32)).astype(BF16)
    h, idx, wts, pos, cnt = _router(x, mod, p['norm2_g'][layer], rw_hi, rw_lo, p['moe_router_b'][layer], cmap_r)
    counts = cnt[:, 0].astype(I32)
    start = jnp.cumsum(counts) - counts
    dest = _dest(idx, pos, start.astype(F32).reshape(-1, 1))
    xs = _dispatch(h, dest)
    items = _work_items(counts, n * TOP_K, TM_EXPERT)
    ys = _experts(xs, items, p['moe_w1'], p['moe_w3'], p['moe_w2'], layer)
    return _combine(x, h, ys, dest, wts.T, mod,
                    p['moe_shared_w1'][layer].astype(BF16), p['moe_shared_w3'][layer].astype(BF16),
                    p['moe_shared_w2'][layer].astype(BF16), p['final_g'], cmap_c, final, split_rows)


def kernel(x_prompt, x_sample, c, state_ssm, cache_k, cache_v, c_ctx, norm1_g, norm2_g, ada_w, ada_b, ssm_lam_re, ssm_lam_im, ssm_log_dt, ssm_b_re, ssm_b_im, ssm_c_re, ssm_c_im, ssm_d, ssm_glu_w, attn_w_qkv, attn_lam_q, attn_lam_k, attn_subln_g, attn_w_o, moe_router_w, moe_router_b, moe_w1, moe_w3, moe_w2, moe_shared_w1, moe_shared_w3, moe_shared_w2, final_g):
    p = dict(norm2_g=norm2_g, moe_router_w=moe_router_w, moe_router_b=moe_router_b,
             moe_w1=moe_w1, moe_w3=moe_w3, moe_w2=moe_w2, moe_shared_w1=moe_shared_w1,
             moe_shared_w3=moe_shared_w3, moe_shared_w2=moe_shared_w2, final_g=final_g)
    b_ctx, l_ctx, d = x_prompt.shape
    b_lat, l_lat, _ = x_sample.shape
    n_ctx, n_lat = b_ctx * l_ctx, b_lat * l_lat
    g = d // SSM_GROUP
    pdim = ssm_lam_re.shape[-1]
    nh = d // V_DIM
    assert 2 * pdim == LANES and CHUNK * SSM_GROUP == 2 * LANES
    assert n_ctx % TM_DENSE == 0 and l_lat % TM_DENSE == 0 and b_lat % 8 == 0 and b_ctx % 8 == 0
    nc_lat, nc_ctx = l_lat // CHUNK, l_ctx // CHUNK
    assert (b_lat * nc_lat) % (b_ctx * nc_ctx) == 0
    cmap = _cond_map(TM_DENSE, n_lat, l_lat)
    cmap_r = _cond_map(TM_ROUTER, n_lat, l_lat)
    cmap_c = _cond_map(TM_ROWS, n_lat, l_lat)

    nc = 1 + b_lat
    nc_pad = -(-nc // 8) * 8
    cond = jnp.concatenate([c_ctx[None, :], c, jnp.zeros((nc_pad - nc, d), F32)], axis=0)
    mod = _ada(cond, ada_w, ada_b)

    x_lat, x_ctx = x_sample.reshape(n_lat, d), x_prompt.reshape(n_ctx, d)

    u = _s5_pack(x_lat, x_ctx, mod[0], norm1_g[0], cmap)
    wts = _s5_weights(ssm_lam_re[0], ssm_lam_im[0], ssm_log_dt[0], ssm_b_re[0], ssm_b_im[0],
                      ssm_c_re[0], ssm_c_im[0])
    h0 = state_ssm[:, 0].transpose(3, 0, 2, 1, 4).reshape(g, b_lat, 2 * LANES)
    y_lat, _ = _s5(u, h0, wts, 0, b_lat, nc_lat)
    y_ctx, fin = _s5(u, jnp.zeros((g, b_ctx, 2 * LANES), F32), wts, b_lat * nc_lat, b_ctx, nc_ctx)
    new_state = fin.reshape(g, b_ctx, 2, 2, pdim).transpose(1, 3, 2, 0, 4)[:, None]
    x = _glu(x_lat, x_ctx, y_lat, y_ctx, mod[0], norm1_g[0], ssm_d[0], ssm_glu_w[0].astype(BF16), cmap)
    x = _moe(x, mod[0], 0, (cmap_r, cmap_c), p, final=False)

    lam_init = 0.8 - 0.6 * math.exp(-0.3 * 1)
    wqkv = attn_w_qkv[0].astype(BF16)
    q_l, k_l, v_l = _qkv(x, mod[1], norm1_g[1], wqkv, cmap, 0, n_lat, rope_l=l_lat)
    q_c, k_c, v_c, kf, vf = _qkv(x, mod[1], norm1_g[1], wqkv, cmap, n_lat, n_ctx)
    new_k = kf.reshape(b_ctx, 1, l_ctx, nh, 2, HEAD_DIM)
    new_v = vf.reshape(b_ctx, 1, l_ctx, nh, V_DIM)
    o_c = _attention(q_c.reshape(b_ctx, l_ctx, d), k_c.reshape(b_ctx, l_ctx, d), v_c.reshape(b_ctx, l_ctx, d),
                     attn_lam_q[0], attn_lam_k[0], attn_subln_g[0], lam_init)
    past = cache_k.shape[2]
    k_all = jnp.concatenate([k_l.reshape(b_lat, l_lat, d), cache_k[:, 0].reshape(b_lat, past, d).astype(BF16)], axis=1)
    v_all = jnp.concatenate([v_l.reshape(b_lat, l_lat, d), cache_v[:, 0].reshape(b_lat, past, d).astype(BF16)], axis=1)
    o_l = _attention(q_l.reshape(b_lat, l_lat, d), k_all, v_all,
                     attn_lam_q[0], attn_lam_k[0], attn_subln_g[0], lam_init)
    x = _oproj(x, o_l.reshape(n_lat, d), o_c.reshape(n_ctx, d), mod[1], attn_w_o[0].astype(BF16), cmap)
    y_lat, y_ctx = _moe(x, mod[1], 1, (cmap_r, cmap_c), p, final=True, split_rows=n_lat)

    return (y_ctx.reshape(b_ctx, l_ctx, d), y_lat.reshape(b_lat, l_lat, d), new_state, new_k, new_v)
```
